```python
import jax, jax.numpy as jnp
from jax import lax
import numpy as np

D_MODEL = 2048
BATCH = 16
SEQ = 256
DEPTH = 4
DEC_BATCH = 4
DEC_SEQ = 2048
PAST_LEN = 256

GRID_W = 64
N_DIR = 2
A_HEAD_DIM = 64
A_WIDTH = D_MODEL // 2
A_HEADS = A_WIDTH // A_HEAD_DIM
DECAY_LORA = 96
ICLR_LORA = 96
DECAY_SCALE = 0.606531
RWKV_GN_EPS = 64e-5
B_HEAD_DIM = 128
B_WIDTH = D_MODEL // 2
B_HEADS = B_WIDTH // B_HEAD_DIM
CONV_K = 3
CHUNK = 64
NORM_EPS = 1e-6
N_IN = 4 * A_WIDTH + 2 * DECAY_LORA + 2 * ICLR_LORA + 4 * B_WIDTH + 2 * N_DIR * B_HEADS + 2 * D_MODEL

kernel_name = "bidir_rwkv7_gdn_flow_backbone"


def rms_norm(x, g, eps=NORM_EPS):
    xf = x.astype(jnp.float32)
    y = xf * lax.rsqrt(jnp.mean(xf * xf, axis=-1, keepdims=True) + eps)
    return (y * g.astype(jnp.float32)).astype(x.dtype)


def l2_normalize(x, eps=NORM_EPS):
    xf = x.astype(jnp.float32)
    return xf * lax.rsqrt(jnp.sum(xf * xf, axis=-1, keepdims=True) + eps)


def grid_transpose(x, rows, cols):
    b, t, d = x.shape
    return x.reshape(b, rows, cols, d).swapaxes(1, 2).reshape(b, t, d)


def centred_depthwise_conv(x, w):
    t = x.shape[1]
    half = CONV_K // 2
    xp = jnp.pad(x, ((0, 0), (half, half), (0, 0)))
    return sum(xp[:, j:j + t] * w[j] for j in range(CONV_K))


def split_in_proj(proj):
    sizes = (A_WIDTH,) * 4 + (DECAY_LORA, DECAY_LORA, ICLR_LORA, ICLR_LORA) + (B_WIDTH,) * 4 + (N_DIR * B_HEADS,) * 2 + (D_MODEL,) * 2
    idx = np.cumsum(sizes)[:-1].tolist()
    return jnp.split(proj, idx, axis=-1)


def rwkv7_scan(r, w, k, v, kk, a, s0, reverse):
    def step(s, inp):
        r_t, w_t, k_t, v_t, kk_t, a_t = inp
        sa = jnp.einsum('bhvk,bhk->bhv', s, -kk_t)
        s = (s * w_t[:, :, None, :]
             + sa[..., None] * (kk_t * a_t)[:, :, None, :]
             + v_t[..., None] * k_t[:, :, None, :])
        return s, jnp.einsum('bhvk,bhk->bhv', s, r_t)
    xs = tuple(jnp.swapaxes(z, 0, 1) for z in (r, w, k, v, kk, a))
    s_fin, ys = lax.scan(step, s0, xs, reverse=reverse)
    return jnp.swapaxes(ys, 0, 1), s_fin


def gated_delta_chunked(q, k, v, g, beta, s0):
    b, t, h, _ = q.shape
    dv = v.shape[-1]
    n = t // CHUNK

    def chunks(z):
        z = z.reshape(b, n, CHUNK, h, *z.shape[3:])
        return jnp.moveaxis(z, (1, 3), (0, 2))

    qc, kc, vc, gc, bc = (chunks(z) for z in (q, k, v, g, beta))
    gcum = jnp.cumsum(gc, axis=-1)
    causal = jnp.tril(jnp.ones((CHUNK, CHUNK), bool))
    strict = jnp.tril(jnp.ones((CHUNK, CHUNK), bool), -1)
    decay = jnp.exp(jnp.where(causal, gcum[..., :, None] - gcum[..., None, :], -jnp.inf))
    kb = kc * bc[..., None]
    lower = jnp.where(strict, jnp.einsum('nbhik,nbhjk->nbhij', kb, kc) * decay, 0.0)
    eye = jnp.eye(CHUNK, dtype=jnp.float32)
    t_inv = lax.linalg.triangular_solve(eye + lower, jnp.broadcast_to(eye, lower.shape),
                                        left_side=True, lower=True, unit_diagonal=True)
    u = t_inv @ (vc * bc[..., None])
    wk = t_inv @ (kb * jnp.exp(gcum)[..., None])
    qk = jnp.einsum('nbhik,nbhjk->nbhij', qc, kc) * decay

    def step(s, inp):
        q_i, k_i, u_i, w_i, g_i, qk_i = inp
        v_new = u_i - w_i @ s
        o = (q_i * jnp.exp(g_i)[..., None]) @ s + qk_i @ v_new
        g_last = g_i[..., -1:]
        s = s * jnp.exp(g_last)[..., None] + jnp.einsum(
            'bhck,bhcv->bhkv', k_i * jnp.exp(g_last - g_i)[..., None], v_new)
        return s, o

    s_fin, o = lax.scan(step, s0, (qc, kc, u, wk, gcum, qk))
    o = jnp.moveaxis(o, (0, 2), (1, 3)).reshape(b, t, h, dv)
    return o, s_fin


def mixer_layer(x, mod, s_rwkv0, s_delta0, lp, grid_rows):
    f32 = jnp.float32
    shift, scale, gate = jnp.split(mod, 3, axis=-1)
    h = rms_norm(x, lp['g_pre']) * (1 + scale[:, None]) + shift[:, None]
    if grid_rows:
        h = grid_transpose(h, grid_rows, GRID_W)
    bsz, t, _ = h.shape
    (r, k, v, z_a, wlo_f, wlo_b, alo_f, alo_b,
     q_b, k_b, v_b, z_b, beta_in, alpha_in, gate_a_in, gate_b_in) = split_in_proj(h @ lp['w_in'])

    def heads_a(z):
        return z.reshape(bsz, t, A_HEADS, A_HEAD_DIM).astype(f32)

    decay_logit = lp['w0'] + jnp.einsum('btdr,drc->btdc', jnp.stack([jnp.tanh(wlo_f), jnp.tanh(wlo_b)], 2), lp['w_up'])
    w_dec = jnp.exp(-DECAY_SCALE * jax.nn.sigmoid(decay_logit.astype(f32)))
    a_rate = jax.nn.sigmoid((lp['a0'] + jnp.einsum('btdr,drc->btdc', jnp.stack([alo_f, alo_b], 2), lp['a_up'])).astype(f32))
    kk = l2_normalize(heads_a(k * lp['k_k']))
    k_dir = k.astype(f32)[:, :, None] * (1 + (a_rate - 1) * lp['k_a'].astype(f32))
    rf, vf = heads_a(r), heads_a(v)
    r_k = lp['r_k'].reshape(A_HEADS, A_HEAD_DIM).astype(f32)
    y_a = 0.0
    s_rwkv = []
    for d, rev in ((0, False), (1, True)):
        kd = heads_a(k_dir[:, :, d])
        y, s_fin = rwkv7_scan(rf, heads_a(w_dec[:, :, d]), kd, vf, kk, heads_a(a_rate[:, :, d]),
                              s_rwkv0[:, d].astype(f32), rev)
        y_a = y_a + y + jnp.sum(rf * kd * r_k, axis=-1, keepdims=True) * vf
        s_rwkv.append(s_fin)
    mu = jnp.mean(y_a, axis=-1, keepdims=True)
    var = jnp.mean(jnp.square(y_a - mu), axis=-1, keepdims=True)
    y_a = ((y_a - mu) * lax.rsqrt(var + RWKV_GN_EPS)).reshape(bsz, t, A_WIDTH)
    y_a = (y_a * lp['gn_w'] + lp['gn_b']).astype(x.dtype) * jax.nn.silu(z_a)
    branch_a = y_a @ lp['w_pa']

    qkv = jax.nn.silu(centred_depthwise_conv(jnp.concatenate([q_b, k_b, v_b], -1), lp['conv_w']))
    qh, kh, vh = (z.reshape(bsz, t, B_HEADS, B_HEAD_DIM) for z in jnp.split(qkv, 3, axis=-1))
    qh = l2_normalize(qh) * (B_HEAD_DIM ** -0.5)
    kh = l2_normalize(kh)
    vh = vh.astype(f32)
    beta = jax.nn.sigmoid(beta_in.reshape(bsz, t, N_DIR, B_HEADS).astype(f32))
    g = -jnp.exp(lp['a_log'].astype(f32)) * jax.nn.softplus(
        alpha_in.reshape(bsz, t, N_DIR, B_HEADS).astype(f32) + lp['dt_bias'].astype(f32))
    o_f, sd_f = gated_delta_chunked(qh, kh, vh, g[:, :, 0], beta[:, :, 0], s_delta0[:, 0].astype(f32))
    flip = lambda z: jnp.flip(z, axis=1)
    o_b, sd_b = gated_delta_chunked(flip(qh), flip(kh), flip(vh), flip(g[:, :, 1]), flip(beta[:, :, 1]),
                                    s_delta0[:, 1].astype(f32))
    o = rms_norm(o_f + flip(o_b), lp['o_norm_w']).reshape(bsz, t, B_WIDTH)
    y_b = o.astype(x.dtype) * jax.nn.silu(z_b)
    branch_b = y_b @ lp['w_pb']

    merged = jax.nn.sigmoid(gate_a_in) * branch_a + jax.nn.sigmoid(gate_b_in) * branch_b
    out = merged @ lp['w_o']
    if grid_rows:
        out = grid_transpose(out, GRID_W, grid_rows)
    x = x + gate[:, None] * rms_norm(out, lp['g_post'])
    return x, jnp.stack(s_rwkv, axis=1), jnp.stack([sd_f, sd_b], axis=1)


def setup_inputs(seed: int = 0) -> dict:
    key = jax.random.key(seed)
    ks = iter(jax.random.split(key, 40))
    f32 = jnp.float32
    D = D_MODEL

    def nrm(shape, s):
        return jax.random.normal(next(ks), shape, f32) * s

    dt = jnp.exp(jax.random.uniform(next(ks), (DEPTH, N_DIR, B_HEADS), f32, np.log(1e-3), np.log(1e-1)))
    return {
        'x_prompt': nrm((BATCH, SEQ, D), 1.0),
        'x_sample': nrm((DEC_BATCH, DEC_SEQ, D), 1.0),
        'state_rwkv': nrm((DEC_BATCH, DEPTH, N_DIR, A_HEADS, A_HEAD_DIM, A_HEAD_DIM), 0.1),
        'state_delta': nrm((DEC_BATCH, DEPTH, N_DIR, B_HEADS, B_HEAD_DIM, B_HEAD_DIM), 0.1),
        'c': nrm((DEC_BATCH, D), 1.0),
        'c_ctx': nrm((D,), 1.0),
        'w_mod': nrm((DEPTH, D, 3 * D), 0.5 * D ** -0.5),
        'b_mod': nrm((DEPTH, 3 * D), 0.01),
        'g_pre': 1.0 + nrm((DEPTH, D), 0.02),
        'g_post': 1.0 + nrm((DEPTH, D), 0.02),
        'w_in': nrm((DEPTH, D, N_IN), D ** -0.5),
        'w0': jax.random.uniform(next(ks), (DEPTH, N_DIR, A_WIDTH), f32, -3.0, 3.0),
        'w_up': nrm((DEPTH, N_DIR, DECAY_LORA, A_WIDTH), 0.1 * DECAY_LORA ** -0.5),
        'a0': nrm((DEPTH, N_DIR, A_WIDTH), 0.5),
        'a_up': nrm((DEPTH, N_DIR, ICLR_LORA, A_WIDTH), 0.1 * ICLR_LORA ** -0.5),
        'k_k': 0.85 + nrm((DEPTH, A_WIDTH), 0.02),
        'k_a': 1.0 + nrm((DEPTH, A_WIDTH), 0.02),
        'r_k': nrm((DEPTH, A_WIDTH), 0.1),
        'gn_w': 1.0 + nrm((DEPTH, A_WIDTH), 0.02),
        'gn_b': nrm((DEPTH, A_WIDTH), 0.01),
        'conv_w': nrm((DEPTH, CONV_K, 3 * B_WIDTH), CONV_K ** -0.5),
        'a_log': jnp.log(jax.random.uniform(next(ks), (DEPTH, N_DIR, B_HEADS), f32, 1.0, 16.0)),
        'dt_bias': dt + jnp.log(-jnp.expm1(-dt)),
        'o_norm_w': 1.0 + nrm((DEPTH, B_HEAD_DIM), 0.02),
        'w_pa': nrm((DEPTH, A_WIDTH, D), A_WIDTH ** -0.5),
        'w_pb': nrm((DEPTH, B_WIDTH, D), B_WIDTH ** -0.5),
        'w_o': nrm((DEPTH, D, D), D ** -0.5),
    }


def reference(x_prompt, x_sample, state_rwkv, state_delta, c, c_ctx, w_mod, b_mod, g_pre, g_post,
              w_in, w0, w_up, a0, a_up, k_k, k_a, r_k, gn_w, gn_b, conv_w, a_log, dt_bias,
              o_norm_w, w_pa, w_pb, w_o):
    rows = x_sample.shape[1] // GRID_W
    bp = x_prompt.shape[0]
    y_prompt, y_sample = x_prompt, x_sample
    zeros_r = jnp.zeros((bp, N_DIR, A_HEADS, A_HEAD_DIM, A_HEAD_DIM), jnp.float32)
    zeros_d = jnp.zeros((bp, N_DIR, B_HEADS, B_HEAD_DIM, B_HEAD_DIM), jnp.float32)
    new_r, new_d = [], []
    for l in range(DEPTH):
        lp = {'g_pre': g_pre[l], 'g_post': g_post[l], 'w_in': w_in[l], 'w0': w0[l], 'w_up': w_up[l],
              'a0': a0[l], 'a_up': a_up[l], 'k_k': k_k[l], 'k_a': k_a[l], 'r_k': r_k[l],
              'gn_w': gn_w[l], 'gn_b': gn_b[l], 'conv_w': conv_w[l], 'a_log': a_log[l],
              'dt_bias': dt_bias[l], 'o_norm_w': o_norm_w[l], 'w_pa': w_pa[l], 'w_pb': w_pb[l],
              'w_o': w_o[l]}
        mod_ctx = (jax.nn.silu(c_ctx) @ w_mod[l] + b_mod[l])[None]
        mod_lat = jax.nn.silu(c) @ w_mod[l] + b_mod[l]
        y_prompt, s_r, s_d = mixer_layer(y_prompt, mod_ctx, zeros_r, zeros_d, lp, 0)
        new_r.append(s_r.astype(x_prompt.dtype))
        new_d.append(s_d.astype(x_prompt.dtype))
        y_sample, _, _ = mixer_layer(y_sample, mod_lat, state_rwkv[:, l], state_delta[:, l], lp,
                                     rows if l % 2 == 1 else 0)
    new_state_rwkv = jnp.stack(new_r, axis=1)
    new_state_delta = jnp.stack(new_d, axis=1)
    return (y_prompt, y_sample, new_state_rwkv, new_state_delta)
```

```python
import functools

import jax
import jax.numpy as jnp
from jax import lax
from jax.experimental import pallas as pl
from jax.experimental.pallas import tpu as pltpu

F32 = jnp.float32
BF16 = jnp.bfloat16

GRID_W = 64
A_HEAD_DIM = 64
B_HEAD_DIM = 128
CHUNK = 64
DECAY_SCALE = 0.606531
RWKV_GN_EPS = 64e-5
NORM_EPS = 1e-6
LANES = 128
VMEM_LIMIT = 56 * 1024 * 1024

NN = (((1,), (0,)), ((), ()))
NT = (((1,), (1,)), ((), ()))
TN = (((0,), (0,)), ((), ()))


def _dot(a, b, dims=NN, hi=False):
    if hi:
        return lax.dot_general(a, b, dims, precision=lax.Precision.HIGHEST, preferred_element_type=F32)
    return lax.dot_general(a.astype(BF16), b.astype(BF16), dims, preferred_element_type=F32)


def _sigmoid(x):
    return jax.nn.sigmoid(x)


def _silu(x):
    return x * jax.nn.sigmoid(x)


def _params(sem):
    return pltpu.CompilerParams(dimension_semantics=sem, vmem_limit_bytes=VMEM_LIMIT)


def _pick(n, cands):
    for c in cands:
        if n % c == 0:
            return c
    raise ValueError(f"no tile for {n}")


def _mod_kernel(c_ref, w_ref, b_ref, o_ref):
    o_ref[0] = _dot(_silu(c_ref[...]), w_ref[0]) + b_ref[0]


def _modulation(cvec, w_mod, b_mod):
    depth, d, n = w_mod.shape
    tn = _pick(n, (512, 256, 128))
    return pl.pallas_call(
        _mod_kernel,
        out_shape=jax.ShapeDtypeStruct((depth, 8, n), F32),
        grid=(depth, n // tn),
        in_specs=[pl.BlockSpec((8, d), lambda l, j: (0, 0)),
                  pl.BlockSpec((1, d, tn), lambda l, j: (l, 0, j)),
                  pl.BlockSpec((1, 1, tn), lambda l, j: (l, 0, j))],
        out_specs=pl.BlockSpec((1, 8, tn), lambda l, j: (l, 0, j)),
        compiler_params=_params(("parallel", "parallel")),
        name="modulation",
    )(cvec, w_mod, b_mod.reshape(depth, 1, n))


def _inproj_kernel(x_ref, mod_ref, g_ref, w_ref, o_ref, h_scr, *, d):
    @pl.when(pl.program_id(2) == 0)
    def _():
        x = x_ref[0]
        y = x * lax.rsqrt(jnp.mean(x * x, axis=-1, keepdims=True) + NORM_EPS) * g_ref[...]
        shift = mod_ref[0, :, 0:d]
        scale = mod_ref[0, :, d:2 * d]
        h_scr[...] = (y * (1.0 + scale) + shift).astype(BF16)

    o_ref[0] = jnp.dot(h_scr[...], w_ref[...], preferred_element_type=F32)


def _in_proj(x, mod, g_pre, w_packed):
    b, t, d = x.shape
    n = w_packed.shape[1]
    tm = _pick(t, (512, 256, 128))
    tn = _pick(n, (1280, 1024, 512, 256, 128))
    return pl.pallas_call(
        functools.partial(_inproj_kernel, d=d),
        out_shape=jax.ShapeDtypeStruct((b, t, n), F32),
        grid=(b, t // tm, n // tn),
        in_specs=[pl.BlockSpec((1, tm, d), lambda bi, i, j: (bi, i, 0)),
                  pl.BlockSpec((1, 1, 3 * d), lambda bi, i, j: (bi, 0, 0)),
                  pl.BlockSpec((1, d), lambda bi, i, j: (0, 0)),
                  pl.BlockSpec((d, tn), lambda bi, i, j: (0, j))],
        out_specs=pl.BlockSpec((1, tm, tn), lambda bi, i, j: (bi, i, j)),
        scratch_shapes=[pltpu.VMEM((tm, d), BF16)],
        compiler_params=_params(("parallel", "parallel", "arbitrary")),
        name="in_proj",
    )(x, mod, g_pre, w_packed)


def _tri_masks(n, period, reverse):
    ti = lax.broadcasted_iota(jnp.int32, (n, n), 0) & (period - 1)
    sj = lax.broadcasted_iota(jnp.int32, (n, n), 1) & (period - 1)
    if reverse:
        return sj > ti, sj >= ti
    return sj < ti, sj <= ti


def _unit_tri_inverse(neg_a, eye, hi):
    t = eye + neg_a
    p = neg_a
    steps = CHUNK.bit_length() - 2
    for _ in range(steps):
        p = _dot(p, p, hi=hi)
        t = t + _dot(t, p, hi=hi)
    return t


def _rwkv_kernel(*refs, reverse, n_chunks, zero_init):
    if zero_init:
        (r_ref, k_ref, v_ref, lora_ref, wup_ref, aup_ref, w0_ref, a0_ref, kkw_ref, kaw_ref, rkw_ref,
         y_ref, sfin_ref, s_scr) = refs
        s0_ref = None
    else:
        (r_ref, k_ref, v_ref, lora_ref, wup_ref, aup_ref, w0_ref, a0_ref, kkw_ref, kaw_ref, rkw_ref,
         s0_ref, y_ref, sfin_ref, s_scr) = refs
    i = pl.program_id(2)
    c = CHUNK
    n2 = 2 * c

    @pl.when(i == 0)
    def _():
        if zero_init:
            s_scr[...] = jnp.zeros_like(s_scr)
        else:
            s_scr[...] = s0_ref[0, 0]

    strict, incl = _tri_masks(n2, c, reverse)
    row = lax.broadcasted_iota(jnp.int32, (n2, n2), 0)
    col = lax.broadcasted_iota(jnp.int32, (n2, n2), 1)
    eye = (row == col).astype(F32)
    seg = ((row // A_HEAD_DIM) == (col // A_HEAD_DIM)).astype(F32)
    _, incl_c = _tri_masks(c, c, reverse)
    mincl = incl_c.astype(F32)
    head0 = lax.broadcasted_iota(jnp.int32, (c, LANES), 1) < A_HEAD_DIM

    def stack(z):
        return jnp.concatenate([jnp.where(head0, z, 0.0), jnp.where(head0, 0.0, z)], axis=0)

    w0 = w0_ref[0]
    a0 = a0_ref[0]
    kkw = kkw_ref[...]
    kaw = kaw_ref[...]
    rkw = rkw_ref[...]
    wup = wup_ref[0]
    aup = aup_ref[0]

    order = range(n_chunks - 1, -1, -1) if reverse else range(n_chunks)
    pre = []
    for ci in order:
        sl = slice(ci * c, (ci + 1) * c)
        r = r_ref[0, sl, :]
        k = k_ref[0, sl, :]
        v = v_ref[0, sl, :]
        lo = lora_ref[0, sl, :]
        logw = -DECAY_SCALE * _sigmoid(w0 + _dot(jnp.tanh(lo), wup))
        a = _sigmoid(a0 + _dot(lo, aup))
        kx = k * kkw
        kk = kx * lax.rsqrt(_dot(kx * kx, seg, hi=True) + NORM_EPS)
        kd = k * (1.0 + (a - 1.0) * kaw)
        bb = kk * a
        bonus = _dot(r * kd * rkw, seg, hi=True) * v
        cs = _dot(mincl, logw, hi=True)
        einv = jnp.exp(-cs)
        rt_s = stack(r * jnp.exp(cs))
        kkt_s = stack(kk * jnp.exp(cs - logw))
        bt_s = stack(bb * einv)
        kt_s = stack(kd * einv)
        v_s = stack(v)
        ptot = jnp.exp(jnp.sum(logw, axis=0, keepdims=True))
        big = _dot(jnp.concatenate([kkt_s, rt_s], axis=0), jnp.concatenate([bt_s, kt_s], axis=0), NT)
        a_b = big[0:n2, 0:n2]
        a_k = big[0:n2, n2:2 * n2]
        rb = big[n2:2 * n2, 0:n2]
        rk = big[n2:2 * n2, n2:2 * n2]
        tinv = _unit_tri_inverse(jnp.where(strict, -a_b, 0.0), eye, hi=True)
        wt = _dot(tinv, kkt_s, hi=True)
        u0 = _dot(tinv, _dot(jnp.where(strict, a_k, 0.0), v_s), hi=True)
        o0 = _dot(jnp.where(incl, rk, 0.0), v_s)
        ds = _dot(v_s, kt_s, TN)
        rbm = jnp.where(incl, rb, 0.0)
        pre.append((sl, wt, u0, o0, ds, rbm, rt_s, bt_s, ptot, bonus))

    s = s_scr[...]
    for (sl, wt, u0, o0, ds, rbm, rt_s, bt_s, ptot, bonus) in pre:
        u = -(_dot(wt, s, NT) + u0)
        o_st = _dot(rt_s, s, NT) + _dot(rbm, u) + o0
        y_ref[0, sl, :] = o_st[0:c] + o_st[c:n2] + bonus
        s = (s + _dot(u, bt_s, TN) + ds) * ptot
    s_scr[...] = s

    @pl.when(i == pl.num_programs(2) - 1)
    def _():
        sfin_ref[0, 0] = s


def _rwkv_scan(proj, d, reverse, wup_pad, aup_pad, w0, a0, kkw, kaw, rkw, s0, *, a_width, lora_off):
    b, t, _ = proj.shape
    n_pairs = a_width // LANES
    tb = _pick(t, (256, 128, 64))
    nblk = t // tb
    lw = wup_pad.shape[1]
    ab = a_width // LANES

    def tix(i):
        return (nblk - 1 - i) if reverse else i

    def seg_spec(k):
        return pl.BlockSpec((1, tb, LANES), lambda bi, p, i: (bi, tix(i), k * ab + p))

    vec_spec = pl.BlockSpec((1, LANES), lambda bi, p, i: (0, p))
    dvec_spec = pl.BlockSpec((1, 1, LANES), lambda bi, p, i: (d, 0, p))
    up_spec = pl.BlockSpec((1, lw, LANES), lambda bi, p, i: (d, 0, p))
    st_spec = pl.BlockSpec((1, 1, LANES, LANES), lambda bi, p, i: (bi, p, 0, 0))
    in_specs = [seg_spec(0), seg_spec(1), seg_spec(2),
                pl.BlockSpec((1, tb, lw), lambda bi, p, i: (bi, tix(i), lora_off // lw)),
                up_spec, up_spec, dvec_spec, dvec_spec, vec_spec, vec_spec, vec_spec]
    args = [proj, proj, proj, proj, wup_pad, aup_pad, w0, a0, kkw, kaw, rkw]
    if s0 is not None:
        in_specs.append(st_spec)
        args.append(s0)
    return pl.pallas_call(
        functools.partial(_rwkv_kernel, reverse=reverse, n_chunks=tb // CHUNK, zero_init=s0 is None),
        out_shape=(jax.ShapeDtypeStruct((b, t, a_width), F32),
                   jax.ShapeDtypeStruct((b, n_pairs, LANES, LANES), F32)),
        grid=(b, n_pairs, nblk),
        in_specs=in_specs,
        out_specs=(pl.BlockSpec((1, tb, LANES), lambda bi, p, i: (bi, tix(i), p)), st_spec),
        scratch_shapes=[pltpu.VMEM((LANES, LANES), F32)],
        compiler_params=_params(("parallel", "parallel", "arbitrary")),
        name="rwkv_bwd" if reverse else "rwkv_fwd",
    )(*args)


def _gdn_prep_kernel(x_ref, cw_ref, o_ref, *, n_q, n_qk):
    j = pl.program_id(1)
    x = x_ref[0]
    t = x.shape[0]
    rowi = lax.broadcasted_iota(jnp.int32, x.shape, 0)
    xm = jnp.where(rowi == 0, 0.0, pltpu.roll(x, 1, 0))
    xp = jnp.where(rowi == t - 1, 0.0, pltpu.roll(x, t - 1, 0))
    y = _silu(xm * cw_ref[0:1, :] + x * cw_ref[1:2, :] + xp * cw_ref[2:3, :])
    nrm = y * lax.rsqrt(jnp.sum(y * y, axis=-1, keepdims=True) + NORM_EPS)
    nrm = nrm * jnp.where(j < n_q, B_HEAD_DIM ** -0.5, 1.0)
    o_ref[0] = jnp.where(j < n_qk, nrm, y)


def _gdn_prep(proj, conv_w, *, b_width, qkv_off):
    b, t, _ = proj.shape
    nh = b_width // LANES
    return pl.pallas_call(
        functools.partial(_gdn_prep_kernel, n_q=nh, n_qk=2 * nh),
        out_shape=jax.ShapeDtypeStruct((b, t, 3 * b_width), F32),
        grid=(b, 3 * nh),
        in_specs=[pl.BlockSpec((1, t, LANES), lambda bi, j: (bi, 0, qkv_off // LANES + j)),
                  pl.BlockSpec((3, LANES), lambda bi, j: (0, j))],
        out_specs=pl.BlockSpec((1, t, LANES), lambda bi, j: (bi, 0, j)),
        compiler_params=_params(("parallel", "parallel")),
        name="gdn_prep",
    )(proj, conv_w)


def _softplus(x):
    return jnp.maximum(x, 0.0) + jnp.log1p(jnp.exp(-jnp.abs(x)))


def _gdn_kernel(*refs, reverse, n_chunks, zero_init, d, n_heads):
    if zero_init:
        q_ref, k_ref, v_ref, tail_ref, avec_ref, dvec_ref, o_ref, sfin_ref, s_scr = refs
        s0_ref = None
    else:
        q_ref, k_ref, v_ref, tail_ref, avec_ref, dvec_ref, s0_ref, o_ref, sfin_ref, s_scr = refs
    h = pl.program_id(1)
    i = pl.program_id(2)
    c = CHUNK

    @pl.when(i == 0)
    def _():
        if zero_init:
            s_scr[...] = jnp.zeros_like(s_scr)
        else:
            s_scr[...] = s0_ref[0, 0]

    strict, incl = _tri_masks(c, c, reverse)
    row = lax.broadcasted_iota(jnp.int32, (c, c), 0)
    col = lax.broadcasted_iota(jnp.int32, (c, c), 1)
    eye = (row == col).astype(F32)
    mincl = incl.astype(F32)
    s_after_j = (row < col) if reverse else (row > col)

    tail = tail_ref[0]
    tb = tail.shape[0]
    lane = lax.broadcasted_iota(jnp.int32, (tb, LANES), 1)
    beta_all = _sigmoid(tail)
    g_all = -jnp.exp(avec_ref[...]) * _softplus(tail + dvec_ref[...])
    beta_col = jnp.sum(jnp.where(lane == d * n_heads + h, beta_all, 0.0), axis=-1, keepdims=True)
    g_col = jnp.sum(jnp.where(lane == (2 + d) * n_heads + h, g_all, 0.0), axis=-1, keepdims=True)

    order = range(n_chunks - 1, -1, -1) if reverse else range(n_chunks)
    pre = []
    for ci in order:
        sl = slice(ci * c, (ci + 1) * c)
        q = q_ref[0, sl, :]
        k = k_ref[0, sl, :]
        v = v_ref[0, sl, :]
        g = g_col[sl]
        beta = beta_col[sl]
        diff = _dot(mincl, jnp.where(s_after_j, jnp.broadcast_to(g, (c, c)), 0.0), hi=True)
        decay = jnp.where(incl, jnp.exp(diff), 0.0)
        gb = jnp.broadcast_to(g, (c, LANES))
        gcum = _dot(mincl, gb, hi=True)
        glast = jnp.sum(gb, axis=0, keepdims=True)
        kb = k * beta
        qkk = _dot(jnp.concatenate([kb, q], axis=0), k, NT)
        lower = jnp.where(strict, qkk[0:c] * decay, 0.0)
        qk = qkk[c:2 * c] * decay
        tinv = _unit_tri_inverse(-lower, eye, hi=True)
        egc = jnp.exp(gcum)
        u = _dot(tinv, v * beta, hi=True)
        wk = _dot(tinv, kb * egc, hi=True)
        qg = q * egc
        kg = k * jnp.exp(glast - gcum)
        pre.append((sl, u, wk, qk, qg, kg, jnp.exp(glast)))

    s = s_scr[...]
    for (sl, u, wk, qk, qg, kg, eg) in pre:
        v_new = u - _dot(wk, s)
        o_ref[0, sl, :] = _dot(qg, s) + _dot(qk, v_new)
        s = s * eg + _dot(kg, v_new, TN)
    s_scr[...] = s

    @pl.when(i == pl.num_programs(2) - 1)
    def _():
        sfin_ref[0, 0] = s


def _gdn_scan(qkv, proj, d, reverse, avec, dvec, s0, *, b_width, tail_off):
    b, t, _ = qkv.shape
    nh = b_width // LANES
    tb = _pick(t, (256, 128, 64))
    nblk = t // tb

    def tix(i):
        return (nblk - 1 - i) if reverse else i

    def seg_spec(k):
        return pl.BlockSpec((1, tb, LANES), lambda bi, h, i: (bi, tix(i), k * nh + h))

    vec_spec = pl.BlockSpec((1, LANES), lambda bi, h, i: (0, 0))
    st_spec = pl.BlockSpec((1, 1, LANES, LANES), lambda bi, h, i: (bi, h, 0, 0))
    in_specs = [seg_spec(0), seg_spec(1), seg_spec(2),
                pl.BlockSpec((1, tb, LANES), lambda bi, h, i: (bi, tix(i), tail_off // LANES)),
                vec_spec, vec_spec]
    args = [qkv, qkv, qkv, proj, avec, dvec]
    if s0 is not None:
        in_specs.append(st_spec)
        args.append(s0)
    return pl.pallas_call(
        functools.partial(_gdn_kernel, reverse=reverse, n_chunks=tb // CHUNK, zero_init=s0 is None, d=d, n_heads=nh),
        out_shape=(jax.ShapeDtypeStruct((b, t, b_width), F32),
                   jax.ShapeDtypeStruct((b, nh, LANES, LANES), F32)),
        grid=(b, nh, nblk),
        in_specs=in_specs,
        out_specs=(pl.BlockSpec((1, tb, LANES), lambda bi, h, i: (bi, tix(i), h)), st_spec),
        scratch_shapes=[pltpu.VMEM((LANES, LANES), F32)],
        compiler_params=_params(("parallel", "parallel", "arbitrary")),
        name="gdn_bwd" if reverse else "gdn_fwd",
    )(*args)


def _post_kernel(yf_ref, yb_ref, za_ref, of_ref, ob_ref, zb_ref, ga_ref, gb_ref, x_ref, mod_ref,
                 gnw_ref, gnb_ref, onw_ref, gpost_ref, wpa_ref, wpb_ref, wo_ref, o_ref, ya_scr, yb_scr, *, d):
    a_width = yf_ref.shape[-1]
    b_width = of_ref.shape[-1]
    row = lax.broadcasted_iota(jnp.int32, (LANES, LANES), 0)
    col = lax.broadcasted_iota(jnp.int32, (LANES, LANES), 1)
    seg_mean = ((row // A_HEAD_DIM) == (col // A_HEAD_DIM)).astype(F32) * (1.0 / A_HEAD_DIM)
    for p in range(a_width // LANES):
        sl = slice(p * LANES, (p + 1) * LANES)
        y = yf_ref[0, :, sl] + yb_ref[0, :, sl]
        cen = y - _dot(y, seg_mean, hi=True)
        var = _dot(cen * cen, seg_mean, hi=True)
        yn = cen * lax.rsqrt(var + RWKV_GN_EPS) * gnw_ref[:, sl] + gnb_ref[:, sl]
        ya_scr[:, sl] = (yn * _silu(za_ref[0, :, sl])).astype(BF16)
    for hh in range(b_width // LANES):
        sl = slice(hh * LANES, (hh + 1) * LANES)
        o = of_ref[0, :, sl] + ob_ref[0, :, sl]
        on = o * lax.rsqrt(jnp.mean(o * o, axis=-1, keepdims=True) + NORM_EPS) * onw_ref[:, sl]
        yb_scr[:, sl] = (on * _silu(zb_ref[0, :, sl])).astype(BF16)
    branch_a = jnp.dot(ya_scr[...], wpa_ref[...], preferred_element_type=F32)
    branch_b = jnp.dot(yb_scr[...], wpb_ref[...], preferred_element_type=F32)
    merged = _sigmoid(ga_ref[0]) * branch_a + _sigmoid(gb_ref[0]) * branch_b
    out = jnp.dot(merged.astype(BF16), wo_ref[...], preferred_element_type=F32)
    on = out * lax.rsqrt(jnp.mean(out * out, axis=-1, keepdims=True) + NORM_EPS) * gpost_ref[...]
    o_ref[0] = x_ref[0] + mod_ref[0, :, 2 * d:3 * d] * on


def _post(yf, yb, of, ob, proj, x, mod, gnw, gnb, onw, gpost, wpa, wpb, wo, *, a_width, b_width):
    b, t, d = x.shape
    tm = _pick(t, (256, 128, 64))
    za_blk = 3
    zb_blk = (4 * a_width + 3 * b_width) // b_width
    ga_blk = (4 * a_width + 4 * b_width) // d
    assert (4 * a_width + 3 * b_width) % b_width == 0 and (4 * a_width + 4 * b_width) % d == 0

    def row_spec(width, blk=0):
        return pl.BlockSpec((1, tm, width), lambda bi, i: (bi, i, blk))

    def full(shape):
        return pl.BlockSpec(shape, lambda bi, i: (0,) * len(shape), pipeline_mode=pl.Buffered(1))

    return pl.pallas_call(
        functools.partial(_post_kernel, d=d),
        out_shape=jax.ShapeDtypeStruct((b, t, d), F32),
        grid=(b, t // tm),
        in_specs=[row_spec(a_width), row_spec(a_width), row_spec(a_width, za_blk),
                  row_spec(b_width), row_spec(b_width), row_spec(b_width, zb_blk),
                  row_spec(d, ga_blk), row_spec(d, ga_blk + 1), row_spec(d),
                  pl.BlockSpec((1, 1, 3 * d), lambda bi, i: (bi, 0, 0)),
                  full((1, a_width)), full((1, a_width)), full((1, b_width)), full((1, d)),
                  full((a_width, d)), full((b_width, d)), full((d, d))],
        out_specs=row_spec(d),
        scratch_shapes=[pltpu.VMEM((tm, a_width), BF16), pltpu.VMEM((tm, b_width), BF16)],
        compiler_params=_params(("parallel", "parallel")),
        name="post",
    )(yf, yb, proj, of, ob, proj, proj, proj, x, mod, gnw, gnb, onw, gpost, wpa, wpb, wo)


def _grid_transpose(x, rows, cols):
    b, t, d = x.shape
    return x.reshape(b, rows, cols, d).swapaxes(1, 2).reshape(b, t, d)


def _pair_block_diag(s):
    b, h, n, _ = s.shape
    sp = s.reshape(b, h // 2, 2, n, n)
    eye2 = jnp.eye(2, dtype=s.dtype)
    return jnp.einsum('bpivk,ij->bpivjk', sp, eye2).reshape(b, h // 2, 2 * n, 2 * n)


def _pair_diag_blocks(s, n):
    b, p = s.shape[:2]
    s6 = s.reshape(b, p, 2, n, 2, n)
    return jnp.stack([s6[:, :, 0, :, 0, :], s6[:, :, 1, :, 1, :]], axis=2).reshape(b, 2 * p, n, n)


def _mixer_layer(x, mod, s_rwkv0, s_delta0, lp, dims):
    a_width, b_width = dims['a_width'], dims['b_width']
    proj = _in_proj(x, mod, lp['g_pre'], lp['w_in'])
    ys, srs, os_, sds = [], [], [], []
    qkv = _gdn_prep(proj, lp['conv_w'], b_width=b_width, qkv_off=4 * a_width)
    for d, rev in ((0, False), (1, True)):
        y, sr = _rwkv_scan(proj, d, rev, lp['wup_pad'], lp['aup_pad'], lp['w0'], lp['a0'], lp['k_k'], lp['k_a'],
                           lp['r_k'], None if s_rwkv0 is None else s_rwkv0[:, d],
                           a_width=a_width, lora_off=dims['lora_off'])
        o, sd = _gdn_scan(qkv, proj, d, rev, lp['avec'], lp['dvec'],
                          None if s_delta0 is None else s_delta0[:, d],
                          b_width=b_width, tail_off=dims['tail_off'])
        ys.append(y)
        srs.append(sr)
        os_.append(o)
        sds.append(sd)
    x_new = _post(ys[0], ys[1], os_[0], os_[1], proj, x, mod, lp['gn_w'], lp['gn_b'], lp['onw'], lp['g_post'],
                  lp['w_pa'], lp['w_pb'], lp['w_o'], a_width=a_width, b_width=b_width)
    return x_new, srs, sds


def kernel(x_prompt, x_sample, state_rwkv, state_delta, c, c_ctx, w_mod, b_mod, g_pre, g_post, w_in, w0, w_up, a0,
           a_up, k_k, k_a, r_k, gn_w, gn_b, conv_w, a_log, dt_bias, o_norm_w, w_pa, w_pb, w_o):
    bp, tp, dm = x_prompt.shape
    bs, ts, _ = x_sample.shape
    depth = w_mod.shape[0]
    a_width = k_k.shape[-1]
    n_bh = a_log.shape[-1]
    b_width = o_norm_w.shape[-1] * n_bh
    lora = w_up.shape[2]
    n_dir = 2
    rows = ts // GRID_W
    assert o_norm_w.shape[-1] == B_HEAD_DIM and a_width % LANES == 0 and 4 * n_bh <= LANES

    sizes = (a_width,) * 4 + (lora,) * 4 + (b_width,) * 4 + (n_dir * n_bh,) * 2 + (dm,) * 2
    offs = [0]
    for sz in sizes:
        offs.append(offs[-1] + sz)
    lora_w = 4 * lora
    lora_off = 4 * a_width + 4 * b_width + 2 * dm
    tail_off = lora_off + lora_w
    tail_used = 2 * n_dir * n_bh
    assert lora_off % lora_w == 0 and tail_off % LANES == 0
    dims = dict(a_width=a_width, b_width=b_width, lora_off=lora_off, tail_off=tail_off)

    def pack_w_in(w):
        pad = jnp.zeros((w.shape[0], LANES - tail_used), w.dtype)
        return jnp.concatenate([w[:, offs[0]:offs[4]], w[:, offs[8]:offs[12]], w[:, offs[14]:offs[16]],
                                w[:, offs[4]:offs[8]], w[:, offs[12]:offs[14]], pad], axis=1).astype(BF16)

    def pad_up(w, first):
        z = jnp.zeros((n_dir, lora_w, a_width), w.dtype)
        for d in range(n_dir):
            z = z.at[d, (first + d) * lora:(first + d + 1) * lora].set(w[d])
        return z.astype(BF16)

    def tail_vec(v):
        z = jnp.zeros((1, LANES), F32)
        return z.at[0, n_dir * n_bh:2 * n_dir * n_bh].set(v.reshape(-1))

    cvec = jnp.concatenate([c_ctx[None], c, jnp.zeros((8 - 1 - bs, dm), F32)], axis=0)
    mods = _modulation(cvec, w_mod, b_mod)

    xp = x_prompt.reshape(1, bp * tp, dm)
    xs = x_sample
    new_r, new_d = [], []
    for l in range(depth):
        lp = dict(
            g_pre=g_pre[l][None], g_post=g_post[l][None], w_in=pack_w_in(w_in[l]),
            wup_pad=pad_up(w_up[l], 0), aup_pad=pad_up(a_up[l], 2),
            w0=w0[l][:, None], a0=a0[l][:, None], k_k=k_k[l][None], k_a=k_a[l][None], r_k=r_k[l][None],
            gn_w=gn_w[l][None], gn_b=gn_b[l][None], conv_w=conv_w[l],
            avec=tail_vec(a_log[l]), dvec=tail_vec(dt_bias[l]),
            onw=jnp.tile(o_norm_w[l], n_bh)[None],
            w_pa=w_pa[l].astype(BF16), w_pb=w_pb[l].astype(BF16), w_o=w_o[l].astype(BF16))
        mod_ctx = mods[l, 0:1][None]
        xp3 = xp.reshape(bp, tp, dm)
        xp_new, srs, sds = _mixer_layer_ctx(xp3, mod_ctx, lp, dims)
        xp = xp_new
        new_r.append(jnp.stack([_pair_diag_blocks(s, A_HEAD_DIM) for s in srs], axis=1))
        new_d.append(jnp.stack(sds, axis=1))
        mod_lat = mods[l, 1:1 + bs][:, None]
        s_r0 = jnp.stack([_pair_block_diag(state_rwkv[:, l, d]) for d in range(n_dir)], axis=1)
        s_d0 = state_delta[:, l]
        if l % 2 == 1:
            xs = _grid_transpose(xs, rows, GRID_W)
        xs, _, _ = _mixer_layer(xs, mod_lat, s_r0, s_d0, lp, dims)
        if l % 2 == 1:
            xs = _grid_transpose(xs, GRID_W, rows)
    y_prompt = xp.reshape(bp, tp, dm)
    new_state_rwkv = jnp.stack(new_r, axis=1)
    new_state_delta = jnp.stack(new_d, axis=1)
    return (y_prompt, xs, new_state_rwkv, new_state_delta)


def _mixer_layer_ctx(x, mod, lp, dims):
    b, t, d = x.shape
    a_width, b_width = dims['a_width'], dims['b_width']
    proj = _in_proj(x.reshape(1, b * t, d), mod, lp['g_pre'], lp['w_in']).reshape(b, t, -1)
    qkv = _gdn_prep(proj, lp['conv_w'], b_width=b_width, qkv_off=4 * a_width)
    ys, srs, os_, sds = [], [], [], []
    for dd, rev in ((0, False), (1, True)):
        y, sr = _rwkv_scan(proj, dd, rev, lp['wup_pad'], lp['aup_pad'], lp['w0'], lp['a0'], lp['k_k'], lp['k_a'],
                           lp['r_k'], None, a_width=a_width, lora_off=dims['lora_off'])
        o, sd = _gdn_scan(qkv, proj, dd, rev, lp['avec'], lp['dvec'], None,
                          b_width=b_width, tail_off=dims['tail_off'])
        ys.append(y)
        srs.append(sr)
        os_.append(o)
        sds.append(sd)
    flat = lambda z: z.reshape(1, b * t, z.shape[-1])
    x_new = _post(flat(ys[0]), flat(ys[1]), flat(os_[0]), flat(os_[1]), flat(proj), flat(x), mod,
                  lp['gn_w'], lp['gn_b'], lp['onw'], lp['g_post'], lp['w_pa'], lp['w_pb'], lp['w_o'],
                  a_width=a_width, b_width=b_width)
    return x_new.reshape(b, t, d), srs, sds
```

```python
import functools

import jax
import jax.numpy as jnp
from jax import lax
from jax.experimental import pallas as pl
from jax.experimental.pallas import tpu as pltpu

F32 = jnp.float32
BF16 = jnp.bfloat16

GRID_W = 64
A_HEAD_DIM = 64
B_HEAD_DIM = 128
CHUNK = 64
TRI_BASE = 16
DECAY_SCALE = 0.606531
RWKV_GN_EPS = 64e-5
NORM_EPS = 1e-6
LANES = 128
VMEM_LIMIT = 56 * 1024 * 1024

NN = (((1,), (0,)), ((), ()))
NT = (((1,), (1,)), ((), ()))
TN = (((0,), (0,)), ((), ()))


def _dot(a, b, dims=NN, hi=False):
    if hi:
        return lax.dot_general(a, b, dims, precision=lax.Precision.HIGHEST, preferred_element_type=F32)
    return lax.dot_general(a.astype(BF16), b.astype(BF16), dims, preferred_element_type=F32)


def _sigmoid(x):
    return jax.nn.sigmoid(x)


def _silu(x):
    return x * jax.nn.sigmoid(x)


def _params(sem):
    return pltpu.CompilerParams(dimension_semantics=sem, vmem_limit_bytes=VMEM_LIMIT)


def _pick(n, cands):
    for c in cands:
        if n % c == 0:
            return c
    raise ValueError(f"no tile for {n}")


def _mod_kernel(c_ref, w_ref, b_ref, o_ref):
    o_ref[0] = _dot(_silu(c_ref[...]), w_ref[0]) + b_ref[0]


def _modulation(cvec, w_mod, b_mod):
    depth, d, n = w_mod.shape
    tn = _pick(n, (512, 256, 128))
    return pl.pallas_call(
        _mod_kernel,
        out_shape=jax.ShapeDtypeStruct((depth, 8, n), F32),
        grid=(depth, n // tn),
        in_specs=[pl.BlockSpec((8, d), lambda l, j: (0, 0)),
                  pl.BlockSpec((1, d, tn), lambda l, j: (l, 0, j)),
                  pl.BlockSpec((1, 1, tn), lambda l, j: (l, 0, j))],
        out_specs=pl.BlockSpec((1, 8, tn), lambda l, j: (l, 0, j)),
        compiler_params=_params(("parallel", "parallel")),
        name="modulation",
    )(cvec, w_mod, b_mod.reshape(depth, 1, n))


def _inproj_kernel(x_ref, mod_ref, g_ref, w_ref, o_ref, h_scr, *, d):
    @pl.when(pl.program_id(2) == 0)
    def _():
        x = x_ref[0]
        y = x * lax.rsqrt(jnp.mean(x * x, axis=-1, keepdims=True) + NORM_EPS) * g_ref[...]
        shift = mod_ref[0, :, 0:d]
        scale = mod_ref[0, :, d:2 * d]
        h_scr[...] = (y * (1.0 + scale) + shift).astype(BF16)

    o_ref[0] = jnp.dot(h_scr[...], w_ref[...], preferred_element_type=F32)


def _in_proj(x, mod, g_pre, w_packed):
    b, t, d = x.shape
    n = w_packed.shape[1]
    tm = _pick(t, (512, 256, 128))
    tn = _pick(n, (1280, 1024, 512, 256, 128))
    return pl.pallas_call(
        functools.partial(_inproj_kernel, d=d),
        out_shape=jax.ShapeDtypeStruct((b, t, n), F32),
        grid=(b, t // tm, n // tn),
        in_specs=[pl.BlockSpec((1, tm, d), lambda bi, i, j: (bi, i, 0)),
                  pl.BlockSpec((1, 1, 3 * d), lambda bi, i, j: (bi, 0, 0)),
                  pl.BlockSpec((1, d), lambda bi, i, j: (0, 0)),
                  pl.BlockSpec((d, tn), lambda bi, i, j: (0, j))],
        out_specs=pl.BlockSpec((1, tm, tn), lambda bi, i, j: (bi, i, j)),
        scratch_shapes=[pltpu.VMEM((tm, d), BF16)],
        compiler_params=_params(("parallel", "parallel", "arbitrary")),
        name="in_proj",
    )(x, mod, g_pre, w_packed)


def _tri_masks(n, period, reverse):
    ti = lax.broadcasted_iota(jnp.int32, (n, n), 0) & (period - 1)
    sj = lax.broadcasted_iota(jnp.int32, (n, n), 1) & (period - 1)
    if reverse:
        return sj > ti, sj >= ti
    return sj < ti, sj <= ti


def _unit_tri_inverse(negs, eye):
    n = eye.shape[0]
    row = lax.broadcasted_iota(jnp.int32, (n, n), 0)
    col = lax.broadcasted_iota(jnp.int32, (n, n), 1)

    def same_block(size):
        return (row // size) == (col // size)

    base_mask = same_block(TRI_BASE)
    ps = [jnp.where(base_mask, na, 0.0) for na in negs]
    ts = [eye + p for p in ps]
    steps = TRI_BASE.bit_length() - 2
    ps = [_dot(p, p) for p in ps]
    for i in range(steps):
        if i + 1 < steps:
            tps = [_dot(jnp.concatenate([t, p], axis=0), p) for t, p in zip(ts, ps)]
            ts = [t + tp[0:n] for t, tp in zip(ts, tps)]
            ps = [tp[n:2 * n] for tp in tps]
        else:
            ts = [t + _dot(t, p) for t, p in zip(ts, ps)]
    size = TRI_BASE
    while size < CHUNK:
        cmask = jnp.logical_and(same_block(2 * size), jnp.logical_not(same_block(size)))
        ms = [_dot(t, jnp.where(cmask, na, 0.0)) for t, na in zip(ts, negs)]
        ts = [t + _dot(m, t) for t, m in zip(ts, ms)]
        size *= 2
    return ts


def _masked_cumsum(mask01, x):
    hi = x.astype(BF16)
    lo = (x - hi.astype(F32)).astype(BF16)
    m = mask01.astype(BF16)
    return (lax.dot_general(m, hi, NN, preferred_element_type=F32)
            + lax.dot_general(m, lo, NN, preferred_element_type=F32))


def _rwkv_kernel(*refs, reverse, n_chunks, zero_init):
    if zero_init:
        (r_ref, k_ref, v_ref, lora_ref, wup_ref, aup_ref, w0_ref, a0_ref, kkw_ref, kaw_ref, rkw_ref,
         y_ref, sfin_ref, s_scr) = refs
        s0_ref = None
    else:
        (r_ref, k_ref, v_ref, lora_ref, wup_ref, aup_ref, w0_ref, a0_ref, kkw_ref, kaw_ref, rkw_ref,
         s0_ref, y_ref, sfin_ref, s_scr) = refs
    i = pl.program_id(2)
    c = CHUNK
    n2 = 2 * c

    @pl.when(i == 0)
    def _():
        if zero_init:
            s_scr[...] = jnp.zeros_like(s_scr)
        else:
            s_scr[...] = s0_ref[0, 0]

    strict, incl = _tri_masks(n2, c, reverse)
    row = lax.broadcasted_iota(jnp.int32, (n2, n2), 0)
    col = lax.broadcasted_iota(jnp.int32, (n2, n2), 1)
    eye = (row == col).astype(F32)
    seg = ((row // A_HEAD_DIM) == (col // A_HEAD_DIM)).astype(F32)
    head0 = lax.broadcasted_iota(jnp.int32, (c, LANES), 1) < A_HEAD_DIM

    def stack(z):
        return jnp.concatenate([jnp.where(head0, z, 0.0), jnp.where(head0, 0.0, z)], axis=0)

    r_all = r_ref[0]
    k_all = k_ref[0]
    v_all = v_ref[0]
    lo = lora_ref[0]
    tb = r_all.shape[0]
    logw = -DECAY_SCALE * _sigmoid(w0_ref[0] + _dot(jnp.tanh(lo), wup_ref[0]))
    a = _sigmoid(a0_ref[0] + _dot(lo, aup_ref[0]))
    kx = k_all * kkw_ref[...]
    kd = k_all * (1.0 + (a - 1.0) * kaw_ref[...])
    sums = _dot(jnp.concatenate([kx * kx, r_all * kd * rkw_ref[...]], axis=0), seg)
    kk = kx * lax.rsqrt(sums[0:tb] + NORM_EPS)
    bonus = sums[tb:2 * tb] * v_all
    bb = kk * a
    trow = lax.broadcasted_iota(jnp.int32, (tb, tb), 0)
    tcol = lax.broadcasted_iota(jnp.int32, (tb, tb), 1)
    same_chunk = (trow // c) == (tcol // c)
    cmask = jnp.logical_and(same_chunk, (tcol >= trow) if reverse else (tcol <= trow))
    cs = _masked_cumsum(cmask, logw)
    einv = jnp.exp(-cs)
    rt = r_all * jnp.exp(cs)
    kkt = kk * jnp.exp(cs - logw)
    bt = bb * einv
    kt = kd * einv

    chunks = list(range(n_chunks - 1, -1, -1) if reverse else range(n_chunks))
    sls = [slice(ci * c, (ci + 1) * c) for ci in chunks]
    rt_s = [stack(rt[sl]).astype(BF16) for sl in sls]
    kkt_s = [stack(kkt[sl]).astype(BF16) for sl in sls]
    bt_s = [stack(bt[sl]).astype(BF16) for sl in sls]
    kt_s = [stack(kt[sl]).astype(BF16) for sl in sls]
    v_s = [stack(v_all[sl]).astype(BF16) for sl in sls]
    ptot = [jnp.exp(jnp.sum(logw[sl], axis=0, keepdims=True)) for sl in sls]

    big = [_dot(jnp.concatenate([x, y], axis=0), jnp.concatenate([z, w], axis=0), NT)
           for x, y, z, w in zip(kkt_s, rt_s, bt_s, kt_s)]
    tinv = _unit_tri_inverse([jnp.where(strict, -g[0:n2, 0:n2], 0.0) for g in big], eye)
    av = [_dot(jnp.concatenate([jnp.where(strict, g[0:n2, n2:2 * n2], 0.0),
                                jnp.where(incl, g[n2:2 * n2, n2:2 * n2], 0.0)], axis=0), vs)
          for g, vs in zip(big, v_s)]
    rbm = [jnp.where(incl, g[n2:2 * n2, 0:n2], 0.0).astype(BF16) for g in big]
    wu = [_dot(t, jnp.concatenate([x, y[0:n2].astype(BF16)], axis=1)) for t, x, y in zip(tinv, kkt_s, av)]
    wr = [jnp.concatenate([x[:, 0:n2].astype(BF16), y], axis=0) for x, y in zip(wu, rt_s)]
    u0 = [x[:, n2:2 * n2] for x in wu]
    o0 = [y[n2:2 * n2] for y in av]
    gd = [_dot(jnp.concatenate([x[:, 0:n2], -x[:, n2:2 * n2]], axis=1), z, TN) for x, z in zip(wu, bt_s)]
    dmat = [_dot(vs, ks, TN) + g[n2:2 * n2] for vs, ks, g in zip(v_s, kt_s, gd)]
    gmat = [g[0:n2].astype(BF16) for g in gd]

    s = s_scr[...]
    for j, sl in enumerate(sls):
        xr = _dot(wr[j], s, NT)
        u = -(xr[0:n2] + u0[j])
        o_st = xr[n2:2 * n2] + _dot(rbm[j], u) + o0[j]
        y_ref[0, sl, :] = o_st[0:c] + o_st[c:n2] + bonus[sl]
        s = (s - _dot(s, gmat[j]) + dmat[j]) * ptot[j]
    s_scr[...] = s

    @pl.when(i == pl.num_programs(2) - 1)
    def _():
        sfin_ref[0, 0] = s


def _rwkv_scan(proj, d, reverse, wup_pad, aup_pad, w0, a0, kkw, kaw, rkw, s0, *, a_width, lora_off):
    b, t, _ = proj.shape
    n_pairs = a_width // LANES
    tb = _pick(t, (256, 128, 64))
    nblk = t // tb
    lw = wup_pad.shape[1]
    ab = a_width // LANES

    def tix(i):
        return (nblk - 1 - i) if reverse else i

    def seg_spec(k):
        return pl.BlockSpec((1, tb, LANES), lambda bi, p, i: (bi, tix(i), k * ab + p))

    vec_spec = pl.BlockSpec((1, LANES), lambda bi, p, i: (0, p))
    dvec_spec = pl.BlockSpec((1, 1, LANES), lambda bi, p, i: (d, 0, p))
    up_spec = pl.BlockSpec((1, lw, LANES), lambda bi, p, i: (d, 0, p))
    st_spec = pl.BlockSpec((1, 1, LANES, LANES), lambda bi, p, i: (bi, p, 0, 0))
    in_specs = [seg_spec(0), seg_spec(1), seg_spec(2),
                pl.BlockSpec((1, tb, lw), lambda bi, p, i: (bi, tix(i), lora_off // lw)),
                up_spec, up_spec, dvec_spec, dvec_spec, vec_spec, vec_spec, vec_spec]
    args = [proj, proj, proj, proj, wup_pad, aup_pad, w0, a0, kkw, kaw, rkw]
    if s0 is not None:
        in_specs.append(st_spec)
        args.append(s0)
    return pl.pallas_call(
        functools.partial(_rwkv_kernel, reverse=reverse, n_chunks=tb // CHUNK, zero_init=s0 is None),
        out_shape=(jax.ShapeDtypeStruct((b, t, a_width), F32),
                   jax.ShapeDtypeStruct((b, n_pairs, LANES, LANES), F32)),
        grid=(b, n_pairs, nblk),
        in_specs=in_specs,
        out_specs=(pl.BlockSpec((1, tb, LANES), lambda bi, p, i: (bi, tix(i), p)), st_spec),
        scratch_shapes=[pltpu.VMEM((LANES, LANES), F32)],
        compiler_params=_params(("parallel", "parallel", "arbitrary")),
        name="rwkv_bwd" if reverse else "rwkv_fwd",
    )(*args)


def _gdn_prep_kernel(x_ref, cw_ref, o_ref, *, n_q, n_qk):
    j = pl.program_id(1)
    x = x_ref[0]
    t = x.shape[0]
    rowi = lax.broadcasted_iota(jnp.int32, x.shape, 0)
    xm = jnp.where(rowi == 0, 0.0, pltpu.roll(x, 1, 0))
    xp = jnp.where(rowi == t - 1, 0.0, pltpu.roll(x, t - 1, 0))
    y = _silu(xm * cw_ref[0:1, :] + x * cw_ref[1:2, :] + xp * cw_ref[2:3, :])
    nrm = y * lax.rsqrt(jnp.sum(y * y, axis=-1, keepdims=True) + NORM_EPS)
    nrm = nrm * jnp.where(j < n_q, B_HEAD_DIM ** -0.5, 1.0)
    o_ref[0] = jnp.where(j < n_qk, nrm, y)


def _gdn_prep(proj, conv_w, *, b_width, qkv_off):
    b, t, _ = proj.shape
    nh = b_width // LANES
    return pl.pallas_call(
        functools.partial(_gdn_prep_kernel, n_q=nh, n_qk=2 * nh),
        out_shape=jax.ShapeDtypeStruct((b, t, 3 * b_width), F32),
        grid=(b, 3 * nh),
        in_specs=[pl.BlockSpec((1, t, LANES), lambda bi, j: (bi, 0, qkv_off // LANES + j)),
                  pl.BlockSpec((3, LANES), lambda bi, j: (0, j))],
        out_specs=pl.BlockSpec((1, t, LANES), lambda bi, j: (bi, 0, j)),
        compiler_params=_params(("parallel", "parallel")),
        name="gdn_prep",
    )(proj, conv_w)


def _softplus(x):
    return jnp.maximum(x, 0.0) + jnp.log1p(jnp.exp(-jnp.abs(x)))


def _gdn_kernel(*refs, reverse, n_chunks, zero_init, d, n_heads):
    if zero_init:
        q_ref, k_ref, v_ref, tail_ref, avec_ref, dvec_ref, o_ref, sfin_ref, s_scr = refs
        s0_ref = None
    else:
        q_ref, k_ref, v_ref, tail_ref, avec_ref, dvec_ref, s0_ref, o_ref, sfin_ref, s_scr = refs
    h = pl.program_id(1)
    i = pl.program_id(2)
    c = CHUNK

    @pl.when(i == 0)
    def _():
        if zero_init:
            s_scr[...] = jnp.zeros_like(s_scr)
        else:
            s_scr[...] = s0_ref[0, 0]

    strict, incl = _tri_masks(c, c, reverse)
    row = lax.broadcasted_iota(jnp.int32, (c, c), 0)
    col = lax.broadcasted_iota(jnp.int32, (c, c), 1)
    eye = (row == col).astype(F32)

    tail = tail_ref[0]
    tb = tail.shape[0]
    lane = lax.broadcasted_iota(jnp.int32, (tb, LANES), 1)
    beta_all = _sigmoid(tail)
    g_all = -jnp.exp(avec_ref[...]) * _softplus(tail + dvec_ref[...])
    beta = jnp.sum(jnp.where(lane == d * n_heads + h, beta_all, 0.0), axis=-1, keepdims=True)
    g = jnp.sum(jnp.where(lane == (2 + d) * n_heads + h, g_all, 0.0), axis=-1, keepdims=True)
    gb = jnp.broadcast_to(g, (tb, LANES))
    pos = lax.broadcasted_iota(jnp.int32, (tb, LANES), 0) & (c - 1)
    s_after_j = (pos < lane) if reverse else (pos > lane)
    trow = lax.broadcasted_iota(jnp.int32, (tb, tb), 0)
    tcol = lax.broadcasted_iota(jnp.int32, (tb, tb), 1)
    same_chunk = (trow // c) == (tcol // c)
    cmask = jnp.logical_and(same_chunk, (tcol >= trow) if reverse else (tcol <= trow))
    cum = _masked_cumsum(cmask, jnp.concatenate([jnp.where(s_after_j, gb, 0.0), gb], axis=1))
    diff = cum[:, 0:c]
    gcum = cum[:, LANES:2 * LANES]
    egc = jnp.exp(gcum)
    q_all = q_ref[0]
    k_all = k_ref[0]
    kb = k_all * beta
    vb_kbe = jnp.concatenate([v_ref[0] * beta, kb * egc], axis=1)
    qg = q_all * egc

    chunks = list(range(n_chunks - 1, -1, -1) if reverse else range(n_chunks))
    sls = [slice(ci * c, (ci + 1) * c) for ci in chunks]
    glast = [jnp.sum(gb[sl], axis=0, keepdims=True) for sl in sls]
    kg = [(k_all[sl] * jnp.exp(gl - gcum[sl])).astype(BF16) for sl, gl in zip(sls, glast)]
    eg = [jnp.exp(gl) for gl in glast]
    decay = [jnp.where(incl, jnp.exp(diff[sl]), 0.0) for sl in sls]

    qkk = [_dot(jnp.concatenate([kb[sl], q_all[sl]], axis=0), k_all[sl], NT) for sl in sls]
    tinv = _unit_tri_inverse([jnp.where(strict, -(x[0:c] * dc), 0.0) for x, dc in zip(qkk, decay)], eye)
    qk = [(x[c:2 * c] * dc).astype(BF16) for x, dc in zip(qkk, decay)]
    uw = [_dot(t, vb_kbe[sl]) for t, sl in zip(tinv, sls)]
    wq = [jnp.concatenate([x[:, LANES:2 * LANES], qg[sl]], axis=0).astype(BF16) for x, sl in zip(uw, sls)]
    dg = [_dot(kgj, x, TN) for kgj, x in zip(kg, uw)]

    s = s_scr[...]
    for j, sl in enumerate(sls):
        ws = _dot(wq[j], s)
        v_new = uw[j][:, 0:LANES] - ws[0:c]
        o_ref[0, sl, :] = ws[c:2 * c] + _dot(qk[j], v_new)
        s = s * eg[j] - _dot(dg[j][:, LANES:2 * LANES], s) + dg[j][:, 0:LANES]
    s_scr[...] = s

    @pl.when(i == pl.num_programs(2) - 1)
    def _():
        sfin_ref[0, 0] = s


def _gdn_scan(qkv, proj, d, reverse, avec, dvec, s0, *, b_width, tail_off):
    b, t, _ = qkv.shape
    nh = b_width // LANES
    tb = _pick(t, (256, 128, 64))
    nblk = t // tb

    def tix(i):
        return (nblk - 1 - i) if reverse else i

    def seg_spec(k):
        return pl.BlockSpec((1, tb, LANES), lambda bi, h, i: (bi, tix(i), k * nh + h))

    vec_spec = pl.BlockSpec((1, LANES), lambda bi, h, i: (0, 0))
    st_spec = pl.BlockSpec((1, 1, LANES, LANES), lambda bi, h, i: (bi, h, 0, 0))
    in_specs = [seg_spec(0), seg_spec(1), seg_spec(2),
                pl.BlockSpec((1, tb, LANES), lambda bi, h, i: (bi, tix(i), tail_off // LANES)),
                vec_spec, vec_spec]
    args = [qkv, qkv, qkv, proj, avec, dvec]
    if s0 is not None:
        in_specs.append(st_spec)
        args.append(s0)
    return pl.pallas_call(
        functools.partial(_gdn_kernel, reverse=reverse, n_chunks=tb // CHUNK, zero_init=s0 is None, d=d, n_heads=nh),
        out_shape=(jax.ShapeDtypeStruct((b, t, b_width), F32),
                   jax.ShapeDtypeStruct((b, nh, LANES, LANES), F32)),
        grid=(b, nh, nblk),
        in_specs=in_specs,
        out_specs=(pl.BlockSpec((1, tb, LANES), lambda bi, h, i: (bi, tix(i), h)), st_spec),
        scratch_shapes=[pltpu.VMEM((LANES, LANES), F32)],
        compiler_params=_params(("parallel", "parallel", "arbitrary")),
        name="gdn_bwd" if reverse else "gdn_fwd",
    )(*args)


def _post_kernel(yf_ref, yb_ref, za_ref, of_ref, ob_ref, zb_ref, ga_ref, gb_ref, x_ref, mod_ref,
                 gnw_ref, gnb_ref, onw_ref, gpost_ref, wpa_ref, wpb_ref, wo_ref, o_ref, ya_scr, yb_scr, *, d):
    a_width = yf_ref.shape[-1]
    b_width = of_ref.shape[-1]
    row = lax.broadcasted_iota(jnp.int32, (LANES, LANES), 0)
    col = lax.broadcasted_iota(jnp.int32, (LANES, LANES), 1)
    seg_mean = ((row // A_HEAD_DIM) == (col // A_HEAD_DIM)).astype(F32) * (1.0 / A_HEAD_DIM)
    for p in range(a_width // LANES):
        sl = slice(p * LANES, (p + 1) * LANES)
        y = yf_ref[0, :, sl] + yb_ref[0, :, sl]
        cen = y - _dot(y, seg_mean, hi=True)
        var = _dot(cen * cen, seg_mean, hi=True)
        yn = cen * lax.rsqrt(var + RWKV_GN_EPS) * gnw_ref[:, sl] + gnb_ref[:, sl]
        ya_scr[:, sl] = (yn * _silu(za_ref[0, :, sl])).astype(BF16)
    for hh in range(b_width // LANES):
        sl = slice(hh * LANES, (hh + 1) * LANES)
        o = of_ref[0, :, sl] + ob_ref[0, :, sl]
        on = o * lax.rsqrt(jnp.mean(o * o, axis=-1, keepdims=True) + NORM_EPS) * onw_ref[:, sl]
        yb_scr[:, sl] = (on * _silu(zb_ref[0, :, sl])).astype(BF16)
    branch_a = jnp.dot(ya_scr[...], wpa_ref[...], preferred_element_type=F32)
    branch_b = jnp.dot(yb_scr[...], wpb_ref[...], preferred_element_type=F32)
    merged = _sigmoid(ga_ref[0]) * branch_a + _sigmoid(gb_ref[0]) * branch_b
    out = jnp.dot(merged.astype(BF16), wo_ref[...], preferred_element_type=F32)
    on = out * lax.rsqrt(jnp.mean(out * out, axis=-1, keepdims=True) + NORM_EPS) * gpost_ref[...]
    o_ref[0] = x_ref[0] + mod_ref[0, :, 2 * d:3 * d] * on


def _post(yf, yb, of, ob, proj, x, mod, gnw, gnb, onw, gpost, wpa, wpb, wo, *, a_width, b_width):
    b, t, d = x.shape
    tm = _pick(t, (256, 128, 64))
    za_blk = 3
    zb_blk = (4 * a_width + 3 * b_width) // b_width
    ga_blk = (4 * a_width + 4 * b_width) // d
    assert (4 * a_width + 3 * b_width) % b_width == 0 and (4 * a_width + 4 * b_width) % d == 0

    def row_spec(width, blk=0):
        return pl.BlockSpec((1, tm, width), lambda bi, i: (bi, i, blk))

    def full(shape):
        return pl.BlockSpec(shape, lambda bi, i: (0,) * len(shape), pipeline_mode=pl.Buffered(1))

    return pl.pallas_call(
        functools.partial(_post_kernel, d=d),
        out_shape=jax.ShapeDtypeStruct((b, t, d), F32),
        grid=(b, t // tm),
        in_specs=[row_spec(a_width), row_spec(a_width), row_spec(a_width, za_blk),
                  row_spec(b_width), row_spec(b_width), row_spec(b_width, zb_blk),
                  row_spec(d, ga_blk), row_spec(d, ga_blk + 1), row_spec(d),
                  pl.BlockSpec((1, 1, 3 * d), lambda bi, i: (bi, 0, 0)),
                  full((1, a_width)), full((1, a_width)), full((1, b_width)), full((1, d)),
                  full((a_width, d)), full((b_width, d)), full((d, d))],
        out_specs=row_spec(d),
        scratch_shapes=[pltpu.VMEM((tm, a_width), BF16), pltpu.VMEM((tm, b_width), BF16)],
        compiler_params=_params(("parallel", "parallel")),
        name="post",
    )(yf, yb, proj, of, ob, proj, proj, proj, x, mod, gnw, gnb, onw, gpost, wpa, wpb, wo)


def _grid_transpose(x, rows, cols):
    b, t, d = x.shape
    return x.reshape(b, rows, cols, d).swapaxes(1, 2).reshape(b, t, d)


def _pair_block_diag(s):
    b, h, n, _ = s.shape
    sp = s.reshape(b, h // 2, 2, n, n)
    eye2 = jnp.eye(2, dtype=s.dtype)
    return jnp.einsum('bpivk,ij->bpivjk', sp, eye2).reshape(b, h // 2, 2 * n, 2 * n)


def _pair_diag_blocks(s, n):
    b, p = s.shape[:2]
    s6 = s.reshape(b, p, 2, n, 2, n)
    return jnp.stack([s6[:, :, 0, :, 0, :], s6[:, :, 1, :, 1, :]], axis=2).reshape(b, 2 * p, n, n)


def _mixer_layer(x, mod, s_rwkv0, s_delta0, lp, dims):
    a_width, b_width = dims['a_width'], dims['b_width']
    proj = _in_proj(x, mod, lp['g_pre'], lp['w_in'])
    ys, srs, os_, sds = [], [], [], []
    qkv = _gdn_prep(proj, lp['conv_w'], b_width=b_width, qkv_off=4 * a_width)
    for d, rev in ((0, False), (1, True)):
        y, sr = _rwkv_scan(proj, d, rev, lp['wup_pad'], lp['aup_pad'], lp['w0'], lp['a0'], lp['k_k'], lp['k_a'],
                           lp['r_k'], None if s_rwkv0 is None else s_rwkv0[:, d],
                           a_width=a_width, lora_off=dims['lora_off'])
        o, sd = _gdn_scan(qkv, proj, d, rev, lp['avec'], lp['dvec'],
                          None if s_delta0 is None else s_delta0[:, d],
                          b_width=b_width, tail_off=dims['tail_off'])
        ys.append(y)
        srs.append(sr)
        os_.append(o)
        sds.append(sd)
    x_new = _post(ys[0], ys[1], os_[0], os_[1], proj, x, mod, lp['gn_w'], lp['gn_b'], lp['onw'], lp['g_post'],
                  lp['w_pa'], lp['w_pb'], lp['w_o'], a_width=a_width, b_width=b_width)
    return x_new, srs, sds


def kernel(x_prompt, x_sample, state_rwkv, state_delta, c, c_ctx, w_mod, b_mod, g_pre, g_post, w_in, w0, w_up, a0,
           a_up, k_k, k_a, r_k, gn_w, gn_b, conv_w, a_log, dt_bias, o_norm_w, w_pa, w_pb, w_o):
    bp, tp, dm = x_prompt.shape
    bs, ts, _ = x_sample.shape
    depth = w_mod.shape[0]
    a_width = k_k.shape[-1]
    n_bh = a_log.shape[-1]
    b_width = o_norm_w.shape[-1] * n_bh
    lora = w_up.shape[2]
    n_dir = 2
    rows = ts // GRID_W
    assert o_norm_w.shape[-1] == B_HEAD_DIM and a_width % LANES == 0 and 4 * n_bh <= LANES

    sizes = (a_width,) * 4 + (lora,) * 4 + (b_width,) * 4 + (n_dir * n_bh,) * 2 + (dm,) * 2
    offs = [0]
    for sz in sizes:
        offs.append(offs[-1] + sz)
    lora_w = 4 * lora
    lora_off = 4 * a_width + 4 * b_width + 2 * dm
    tail_off = lora_off + lora_w
    tail_used = 2 * n_dir * n_bh
    assert lora_off % lora_w == 0 and tail_off % LANES == 0
    dims = dict(a_width=a_width, b_width=b_width, lora_off=lora_off, tail_off=tail_off)

    def pack_w_in(w):
        pad = jnp.zeros((w.shape[0], LANES - tail_used), w.dtype)
        return jnp.concatenate([w[:, offs[0]:offs[4]], w[:, offs[8]:offs[12]], w[:, offs[14]:offs[16]],
                                w[:, offs[4]:offs[8]], w[:, offs[12]:offs[14]], pad], axis=1).astype(BF16)

    def pad_up(w, first):
        z = jnp.zeros((n_dir, lora_w, a_width), w.dtype)
        for d in range(n_dir):
            z = z.at[d, (first + d) * lora:(first + d + 1) * lora].set(w[d])
        return z.astype(BF16)

    def tail_vec(v):
        z = jnp.zeros((1, LANES), F32)
        return z.at[0, n_dir * n_bh:2 * n_dir * n_bh].set(v.reshape(-1))

    cvec = jnp.concatenate([c_ctx[None], c, jnp.zeros((8 - 1 - bs, dm), F32)], axis=0)
    mods = _modulation(cvec, w_mod, b_mod)

    xp = x_prompt.reshape(1, bp * tp, dm)
    xs = x_sample
    new_r, new_d = [], []
    for l in range(depth):
        lp = dict(
            g_pre=g_pre[l][None], g_post=g_post[l][None], w_in=pack_w_in(w_in[l]),
            wup_pad=pad_up(w_up[l], 0), aup_pad=pad_up(a_up[l], 2),
            w0=w0[l][:, None], a0=a0[l][:, None], k_k=k_k[l][None], k_a=k_a[l][None], r_k=r_k[l][None],
            gn_w=gn_w[l][None], gn_b=gn_b[l][None], conv_w=conv_w[l],
            avec=tail_vec(a_log[l]), dvec=tail_vec(dt_bias[l]),
            onw=jnp.tile(o_norm_w[l], n_bh)[None],
            w_pa=w_pa[l].astype(BF16), w_pb=w_pb[l].astype(BF16), w_o=w_o[l].astype(BF16))
        mod_ctx = mods[l, 0:1][None]
        xp3 = xp.reshape(bp, tp, dm)
        xp_new, srs, sds = _mixer_layer_ctx(xp3, mod_ctx, lp, dims)
        xp = xp_new
        new_r.append(jnp.stack([_pair_diag_blocks(s, A_HEAD_DIM) for s in srs], axis=1))
        new_d.append(jnp.stack(sds, axis=1))
        mod_lat = mods[l, 1:1 + bs][:, None]
        s_r0 = jnp.stack([_pair_block_diag(state_rwkv[:, l, d]) for d in range(n_dir)], axis=1)
        s_d0 = state_delta[:, l]
        if l % 2 == 1:
            xs = _grid_transpose(xs, rows, GRID_W)
        xs, _, _ = _mixer_layer(xs, mod_lat, s_r0, s_d0, lp, dims)
        if l % 2 == 1:
            xs = _grid_transpose(xs, GRID_W, rows)
    y_prompt = xp.reshape(bp, tp, dm)
    new_state_rwkv = jnp.stack(new_r, axis=1)
    new_state_delta = jnp.stack(new_d, axis=1)
    return (y_prompt, xs, new_state_rwkv, new_state_delta)


def _mixer_layer_ctx(x, mod, lp, dims):
    b, t, d = x.shape
    a_width, b_width = dims['a_width'], dims['b_width']
    proj = _in_proj(x.reshape(1, b * t, d), mod, lp['g_pre'], lp['w_in']).reshape(b, t, -1)
    qkv = _gdn_prep(proj, lp['conv_w'], b_width=b_width, qkv_off=4 * a_width)
    ys, srs, os_, sds = [], [], [], []
    for dd, rev in ((0, False), (1, True)):
        y, sr = _rwkv_scan(proj, dd, rev, lp['wup_pad'], lp['aup_pad'], lp['w0'], lp['a0'], lp['k_k'], lp['k_a'],
                           lp['r_k'], None, a_width=a_width, lora_off=dims['lora_off'])
        o, sd = _gdn_scan(qkv, proj, dd, rev, lp['avec'], lp['dvec'], None,
                          b_width=b_width, tail_off=dims['tail_off'])
        ys.append(y)
        srs.append(sr)
        os_.append(o)
        sds.append(sd)
    flat = lambda z: z.reshape(1, b * t, z.shape[-1])
    x_new = _post(flat(ys[0]), flat(ys[1]), flat(os_[0]), flat(os_[1]), flat(proj), flat(x), mod,
                  lp['gn_w'], lp['gn_b'], lp['onw'], lp['g_post'], lp['w_pa'], lp['w_pb'], lp['w_o'],
                  a_width=a_width, b_width=b_width)
    return x_new.reshape(b, t, d), srs, sds
```

```python
import functools

import jax
import jax.numpy as jnp
from jax import lax
from jax.experimental import pallas as pl
from jax.experimental.pallas import tpu as pltpu

F32 = jnp.float32
BF16 = jnp.bfloat16

GRID_W = 64
A_HEAD_DIM = 64
B_HEAD_DIM = 128
CHUNK = 64
SCAN_TILES = 8
TRI_BASE = 16
DECAY_SCALE = 0.606531
RWKV_GN_EPS = 64e-5
NORM_EPS = 1e-6
LANES = 128
VMEM_LIMIT = 56 * 1024 * 1024
PREP_BLOCK_BYTES = 2 * 1024 * 1024

NN = (((1,), (0,)), ((), ()))
NT = (((1,), (1,)), ((), ()))
TN = (((0,), (0,)), ((), ()))


def _dot(a, b, dims=NN, hi=False):
    if hi:
        return lax.dot_general(a, b, dims, precision=lax.Precision.HIGHEST, preferred_element_type=F32)
    return lax.dot_general(a.astype(BF16), b.astype(BF16), dims, preferred_element_type=F32)


def _sigmoid(x):
    return jax.nn.sigmoid(x)


def _silu(x):
    return x * jax.nn.sigmoid(x)


def _params(sem):
    return pltpu.CompilerParams(dimension_semantics=sem, vmem_limit_bytes=VMEM_LIMIT)


def _pick(n, cands):
    for c in cands:
        if n % c == 0:
            return c
    raise ValueError(f"no tile for {n}")


def _mod_kernel(c_ref, w_ref, b_ref, o_ref):
    o_ref[0] = _dot(_silu(c_ref[...]), w_ref[0]) + b_ref[0]


def _modulation(cvec, w_mod, b_mod):
    depth, d, n = w_mod.shape
    tn = _pick(n, (512, 256, 128))
    return pl.pallas_call(
        _mod_kernel,
        out_shape=jax.ShapeDtypeStruct((depth, 8, n), F32),
        grid=(depth, n // tn),
        in_specs=[pl.BlockSpec((8, d), lambda l, j: (0, 0)),
                  pl.BlockSpec((1, d, tn), lambda l, j: (l, 0, j)),
                  pl.BlockSpec((1, 1, tn), lambda l, j: (l, 0, j))],
        out_specs=pl.BlockSpec((1, 8, tn), lambda l, j: (l, 0, j)),
        compiler_params=_params(("parallel", "parallel")),
        name="modulation",
    )(cvec, w_mod, b_mod.reshape(depth, 1, n))


def _inproj_kernel(x_ref, mod_ref, g_ref, w_ref, o_ref, h_scr, *, d):
    @pl.when(pl.program_id(2) == 0)
    def _():
        x = x_ref[0]
        y = x * lax.rsqrt(jnp.mean(x * x, axis=-1, keepdims=True) + NORM_EPS) * g_ref[...]
        shift = mod_ref[0, :, 0:d]
        scale = mod_ref[0, :, d:2 * d]
        h_scr[...] = (y * (1.0 + scale) + shift).astype(BF16)

    o_ref[0] = jnp.dot(h_scr[...], w_ref[...], preferred_element_type=F32)


def _in_proj(x, mod, g_pre, w_packed):
    b, t, d = x.shape
    n = w_packed.shape[1]
    tm = _pick(t, (1024, 512, 256, 128))
    tn = _pick(n, (1280, 1024, 512, 256, 128))
    return pl.pallas_call(
        functools.partial(_inproj_kernel, d=d),
        out_shape=jax.ShapeDtypeStruct((b, t, n), F32),
        grid=(b, t // tm, n // tn),
        in_specs=[pl.BlockSpec((1, tm, d), lambda bi, i, j: (bi, i, 0)),
                  pl.BlockSpec((1, 1, 3 * d), lambda bi, i, j: (bi, 0, 0)),
                  pl.BlockSpec((1, d), lambda bi, i, j: (0, 0)),
                  pl.BlockSpec((d, tn), lambda bi, i, j: (0, j))],
        out_specs=pl.BlockSpec((1, tm, tn), lambda bi, i, j: (bi, i, j)),
        scratch_shapes=[pltpu.VMEM((tm, d), BF16)],
        compiler_params=_params(("parallel", "parallel", "arbitrary")),
        name="in_proj",
    )(x, mod, g_pre, w_packed)


def _tri_masks(n, period, reverse):
    ti = lax.broadcasted_iota(jnp.int32, (n, n), 0) & (period - 1)
    sj = lax.broadcasted_iota(jnp.int32, (n, n), 1) & (period - 1)
    if reverse:
        return sj > ti, sj >= ti
    return sj < ti, sj <= ti


def _unit_tri_inverse(negs, eye):
    n = eye.shape[0]
    row = lax.broadcasted_iota(jnp.int32, (n, n), 0)
    col = lax.broadcasted_iota(jnp.int32, (n, n), 1)

    def same_block(size):
        return (row // size) == (col // size)

    base_mask = same_block(TRI_BASE)
    ps = [jnp.where(base_mask, na, 0.0) for na in negs]
    ts = [eye + p for p in ps]
    steps = TRI_BASE.bit_length() - 2
    ps = [_dot(p, p) for p in ps]
    for i in range(steps):
        if i + 1 < steps:
            tps = [_dot(jnp.concatenate([t, p], axis=0), p) for t, p in zip(ts, ps)]
            ts = [t + tp[0:n] for t, tp in zip(ts, tps)]
            ps = [tp[n:2 * n] for tp in tps]
        else:
            ts = [t + _dot(t, p) for t, p in zip(ts, ps)]
    size = TRI_BASE
    while size < CHUNK:
        cmask = jnp.logical_and(same_block(2 * size), jnp.logical_not(same_block(size)))
        ms = [_dot(t, jnp.where(cmask, na, 0.0)) for t, na in zip(ts, negs)]
        ts = [t + _dot(m, t) for t, m in zip(ts, ms)]
        size *= 2
    return ts


def _masked_cumsum(mask01, x):
    n = x.shape[1]
    hi = x.astype(BF16)
    lo = (x - hi.astype(F32)).astype(BF16)
    both = lax.dot_general(mask01.astype(BF16), jnp.concatenate([hi, lo], axis=1), NN, preferred_element_type=F32)
    return both[:, 0:n] + both[:, n:2 * n]


def _rwkv_kernel(*refs, reverse, n_chunks, zero_init, n_tiles):
    if zero_init:
        (r_ref, k_ref, v_ref, lora_ref, wup_ref, aup_ref, w0_ref, a0_ref, kkw_ref, kaw_ref, rkw_ref,
         y_ref, sfin_ref, s_scr) = refs
        s0_ref = None
    else:
        (r_ref, k_ref, v_ref, lora_ref, wup_ref, aup_ref, w0_ref, a0_ref, kkw_ref, kaw_ref, rkw_ref,
         s0_ref, y_ref, sfin_ref, s_scr) = refs
    i = pl.program_id(2)
    c = CHUNK
    n2 = 2 * c
    width = n_tiles * LANES

    @pl.when(i == 0)
    def _():
        if zero_init:
            s_scr[...] = jnp.zeros_like(s_scr)
        else:
            s_scr[...] = s0_ref[0]

    strict, incl = _tri_masks(n2, c, reverse)
    row = lax.broadcasted_iota(jnp.int32, (n2, n2), 0)
    col = lax.broadcasted_iota(jnp.int32, (n2, n2), 1)
    eye = (row == col).astype(F32)
    wrow = lax.broadcasted_iota(jnp.int32, (width, width), 0)
    wcol = lax.broadcasted_iota(jnp.int32, (width, width), 1)
    seg = ((wrow // A_HEAD_DIM) == (wcol // A_HEAD_DIM)).astype(F32)
    head0 = lax.broadcasted_iota(jnp.int32, (c, LANES), 1) < A_HEAD_DIM

    def stack(z):
        return jnp.concatenate([jnp.where(head0, z, 0.0), jnp.where(head0, 0.0, z)], axis=0).astype(BF16)

    r_all = r_ref[0]
    k_all = k_ref[0]
    v_all = v_ref[0]
    lo = lora_ref[0]
    tb = r_all.shape[0]
    logw = -DECAY_SCALE * _sigmoid(w0_ref[0] + _dot(jnp.tanh(lo), wup_ref[0]))
    a = _sigmoid(a0_ref[0] + _dot(lo, aup_ref[0]))
    kx = k_all * kkw_ref[...]
    kd = k_all * (1.0 + (a - 1.0) * kaw_ref[...])
    sums = _dot(jnp.concatenate([kx * kx, r_all * kd * rkw_ref[...]], axis=0), seg)
    kk = kx * lax.rsqrt(sums[0:tb] + NORM_EPS)
    bonus = sums[tb:2 * tb] * v_all
    bb = kk * a
    trow = lax.broadcasted_iota(jnp.int32, (tb, tb), 0)
    tcol = lax.broadcasted_iota(jnp.int32, (tb, tb), 1)
    same_chunk = (trow // c) == (tcol // c)
    cmask = jnp.logical_and(same_chunk, (tcol >= trow) if reverse else (tcol <= trow))
    cs = _masked_cumsum(cmask, logw)
    einv = jnp.exp(-cs)
    rt = r_all * jnp.exp(cs)
    kkt = kk * jnp.exp(cs - logw)
    bt = bb * einv
    kt = kd * einv

    chunks = list(range(n_chunks - 1, -1, -1) if reverse else range(n_chunks))
    probs = [(slice(ci * c, (ci + 1) * c), g, slice(g * LANES, (g + 1) * LANES))
             for ci in chunks for g in range(n_tiles)]
    rt_s = [stack(rt[sl, ln]) for sl, _, ln in probs]
    kkt_s = [stack(kkt[sl, ln]) for sl, _, ln in probs]
    bt_s = [stack(bt[sl, ln]) for sl, _, ln in probs]
    kt_s = [stack(kt[sl, ln]) for sl, _, ln in probs]
    v_s = [stack(v_all[sl, ln]) for sl, _, ln in probs]
    ptot = [jnp.exp(jnp.sum(logw[sl, ln], axis=0, keepdims=True)) for sl, _, ln in probs]

    big = [_dot(jnp.concatenate([x, y], axis=0), jnp.concatenate([z, w], axis=0), NT)
           for x, y, z, w in zip(kkt_s, rt_s, bt_s, kt_s)]
    tinv = _unit_tri_inverse([jnp.where(strict, -g[0:n2, 0:n2], 0.0) for g in big], eye)
    av = [_dot(jnp.concatenate([jnp.where(strict, g[0:n2, n2:2 * n2], 0.0),
                                jnp.where(incl, g[n2:2 * n2, n2:2 * n2], 0.0)], axis=0), vs)
          for g, vs in zip(big, v_s)]
    rbm = [jnp.where(incl, g[n2:2 * n2, 0:n2], 0.0).astype(BF16) for g in big]
    wu = [_dot(t, jnp.concatenate([x, y[0:n2].astype(BF16)], axis=1)) for t, x, y in zip(tinv, kkt_s, av)]
    wr = [jnp.concatenate([x[:, 0:n2].astype(BF16), y], axis=0) for x, y in zip(wu, rt_s)]
    u0 = [x[:, n2:2 * n2] for x in wu]
    o0 = [y[n2:2 * n2] for y in av]
    gd = [_dot(jnp.concatenate([x[:, 0:n2], -x[:, n2:2 * n2]], axis=1), z, TN) for x, z in zip(wu, bt_s)]
    dmat = [_dot(vs, ks, TN) + g[n2:2 * n2] for vs, ks, g in zip(v_s, kt_s, gd)]
    gmat = [g[0:n2].astype(BF16) for g in gd]

    s = [s_scr[g] for g in range(n_tiles)]
    for j, (sl, g, ln) in enumerate(probs):
        xr = _dot(wr[j], s[g], NT)
        u = -(xr[0:n2] + u0[j])
        o_st = xr[n2:2 * n2] + _dot(rbm[j], u) + o0[j]
        y_ref[0, sl, ln] = o_st[0:c] + o_st[c:n2] + bonus[sl, ln]
        s[g] = (s[g] - _dot(s[g], gmat[j]) + dmat[j]) * ptot[j]
    for g in range(n_tiles):
        s_scr[g] = s[g]

    @pl.when(i == pl.num_programs(2) - 1)
    def _():
        for g in range(n_tiles):
            sfin_ref[0, g] = s[g]


def _rwkv_scan(proj, d, reverse, wup_pad, aup_pad, w0, a0, kkw, kaw, rkw, s0, *, a_width, lora_off):
    b, t, _ = proj.shape
    n_pairs = a_width // LANES
    g = _pick(n_pairs, (SCAN_TILES, 1))
    gw = g * LANES
    tb = _pick(t, (256, 128, 64))
    nblk = t // tb
    lw = wup_pad.shape[1]
    ab = a_width // gw

    def tix(i):
        return (nblk - 1 - i) if reverse else i

    def seg_spec(k):
        return pl.BlockSpec((1, tb, gw), lambda bi, p, i: (bi, tix(i), k * ab + p))

    vec_spec = pl.BlockSpec((1, gw), lambda bi, p, i: (0, p))
    dvec_spec = pl.BlockSpec((1, 1, gw), lambda bi, p, i: (d, 0, p))
    up_spec = pl.BlockSpec((1, lw, gw), lambda bi, p, i: (d, 0, p))
    st_spec = pl.BlockSpec((1, g, LANES, LANES), lambda bi, p, i: (bi, p, 0, 0))
    in_specs = [seg_spec(0), seg_spec(1), seg_spec(2),
                pl.BlockSpec((1, tb, lw), lambda bi, p, i: (bi, tix(i), lora_off // lw)),
                up_spec, up_spec, dvec_spec, dvec_spec, vec_spec, vec_spec, vec_spec]
    args = [proj, proj, proj, proj, wup_pad, aup_pad, w0, a0, kkw, kaw, rkw]
    if s0 is not None:
        in_specs.append(st_spec)
        args.append(s0)
    return pl.pallas_call(
        functools.partial(_rwkv_kernel, reverse=reverse, n_chunks=tb // CHUNK, zero_init=s0 is None, n_tiles=g),
        out_shape=(jax.ShapeDtypeStruct((b, t, a_width), F32),
                   jax.ShapeDtypeStruct((b, n_pairs, LANES, LANES), F32)),
        grid=(b, n_pairs // g, nblk),
        in_specs=in_specs,
        out_specs=(pl.BlockSpec((1, tb, gw), lambda bi, p, i: (bi, tix(i), p)), st_spec),
        scratch_shapes=[pltpu.VMEM((g, LANES, LANES), F32)],
        compiler_params=_params(("parallel", "parallel", "arbitrary")),
        name="rwkv_bwd" if reverse else "rwkv_fwd",
    )(*args)


def _gdn_prep_kernel(x_ref, cw_ref, o_ref, *, blocks_per_part):
    part = pl.program_id(1) // blocks_per_part
    t = x_ref.shape[1]
    rowi = lax.broadcasted_iota(jnp.int32, (t, LANES), 0)
    first = rowi == 0
    last = rowi == t - 1
    q_scale = jnp.where(part == 0, B_HEAD_DIM ** -0.5, 1.0)
    for k in range(x_ref.shape[2] // LANES):
        sl = slice(k * LANES, (k + 1) * LANES)
        x = x_ref[0, :, sl]
        xm = jnp.where(first, 0.0, pltpu.roll(x, 1, 0))
        xp = jnp.where(last, 0.0, pltpu.roll(x, t - 1, 0))
        y = _silu(xm * cw_ref[0:1, sl] + x * cw_ref[1:2, sl] + xp * cw_ref[2:3, sl])
        nrm = y * (lax.rsqrt(jnp.sum(y * y, axis=-1, keepdims=True) + NORM_EPS) * q_scale)
        o_ref[0, :, sl] = jnp.where(part < 2, nrm, y)


def _gdn_prep(proj, conv_w, *, b_width, qkv_off):
    b, t, _ = proj.shape
    cw = _pick(b_width, tuple(w for w in (1024, 512, 256, 128) if w * t * 4 <= PREP_BLOCK_BYTES) + (LANES,))
    nblk = b_width // cw
    return pl.pallas_call(
        functools.partial(_gdn_prep_kernel, blocks_per_part=nblk),
        out_shape=jax.ShapeDtypeStruct((b, t, 3 * b_width), F32),
        grid=(b, 3 * nblk),
        in_specs=[pl.BlockSpec((1, t, cw), lambda bi, j: (bi, 0, qkv_off // cw + j)),
                  pl.BlockSpec((3, cw), lambda bi, j: (0, j))],
        out_specs=pl.BlockSpec((1, t, cw), lambda bi, j: (bi, 0, j)),
        compiler_params=_params(("parallel", "parallel")),
        name="gdn_prep",
    )(proj, conv_w)


def _softplus(x):
    return jnp.maximum(x, 0.0) + jnp.log1p(jnp.exp(-jnp.abs(x)))


def _gdn_kernel(*refs, reverse, n_chunks, zero_init, d, n_heads, n_tiles):
    if zero_init:
        q_ref, k_ref, v_ref, tail_ref, avec_ref, dvec_ref, o_ref, sfin_ref, s_scr = refs
        s0_ref = None
    else:
        q_ref, k_ref, v_ref, tail_ref, avec_ref, dvec_ref, s0_ref, o_ref, sfin_ref, s_scr = refs
    h0 = pl.program_id(1) * n_tiles
    i = pl.program_id(2)
    c = CHUNK

    @pl.when(i == 0)
    def _():
        if zero_init:
            s_scr[...] = jnp.zeros_like(s_scr)
        else:
            s_scr[...] = s0_ref[0]

    strict, incl = _tri_masks(c, c, reverse)
    row = lax.broadcasted_iota(jnp.int32, (c, c), 0)
    col = lax.broadcasted_iota(jnp.int32, (c, c), 1)
    eye = (row == col).astype(F32)

    tail = tail_ref[0]
    tb = tail.shape[0]
    lane = lax.broadcasted_iota(jnp.int32, (tb, LANES), 1)
    beta_all = _sigmoid(tail)
    g_all = -jnp.exp(avec_ref[...]) * _softplus(tail + dvec_ref[...])
    pos = lax.broadcasted_iota(jnp.int32, (tb, LANES), 0) & (c - 1)
    keep = jnp.logical_or(lane >= c, (pos < lane) if reverse else (pos > lane))
    trow = lax.broadcasted_iota(jnp.int32, (tb, tb), 0)
    tcol = lax.broadcasted_iota(jnp.int32, (tb, tb), 1)
    same_chunk = (trow // c) == (tcol // c)
    cmask = jnp.logical_and(same_chunk, (tcol >= trow) if reverse else (tcol <= trow))
    beta, gcol, gpack = [], [], []
    for t_ in range(n_tiles):
        h = h0 + t_
        beta.append(jnp.sum(jnp.where(lane == d * n_heads + h, beta_all, 0.0), axis=-1, keepdims=True))
        gcol.append(jnp.sum(jnp.where(lane == (2 + d) * n_heads + h, g_all, 0.0), axis=-1, keepdims=True))
        gpack.append(jnp.where(keep, jnp.broadcast_to(gcol[t_], (tb, LANES)), 0.0))
    cum = _masked_cumsum(cmask, jnp.concatenate(gpack, axis=1))
    q_all = q_ref[0]
    k_all = k_ref[0]
    v_all = v_ref[0]
    diff, gcum, kb, vb_kbe, qg = [], [], [], [], []
    for t_ in range(n_tiles):
        ln = slice(t_ * LANES, (t_ + 1) * LANES)
        diff.append(cum[:, t_ * LANES:t_ * LANES + c])
        gcum.append(jnp.broadcast_to(cum[:, t_ * LANES + c:t_ * LANES + c + 1], (tb, LANES)))
        egc = jnp.exp(gcum[t_])
        kb.append(k_all[:, ln] * beta[t_])
        vb_kbe.append(jnp.concatenate([v_all[:, ln] * beta[t_], kb[t_] * egc], axis=1))
        qg.append(q_all[:, ln] * egc)

    chunks = list(range(n_chunks - 1, -1, -1) if reverse else range(n_chunks))
    probs = [(slice(ci * c, (ci + 1) * c), t_, slice(t_ * LANES, (t_ + 1) * LANES))
             for ci in chunks for t_ in range(n_tiles)]
    glast = [jnp.sum(jnp.broadcast_to(gcol[t_][sl], (c, LANES)), axis=0, keepdims=True) for sl, t_, _ in probs]
    kg = [(k_all[sl, ln] * jnp.exp(gl - gcum[t_][sl])).astype(BF16) for (sl, t_, ln), gl in zip(probs, glast)]
    eg = [jnp.exp(gl) for gl in glast]
    decay = [jnp.where(incl, jnp.exp(diff[t_][sl]), 0.0) for sl, t_, _ in probs]

    qkk = [_dot(jnp.concatenate([kb[t_][sl], q_all[sl, ln]], axis=0), k_all[sl, ln], NT) for sl, t_, ln in probs]
    tinv = _unit_tri_inverse([jnp.where(strict, -(x[0:c] * dc), 0.0) for x, dc in zip(qkk, decay)], eye)
    qk = [(x[c:2 * c] * dc).astype(BF16) for x, dc in zip(qkk, decay)]
    uw = [_dot(t, vb_kbe[t_][sl]) for t, (sl, t_, _) in zip(tinv, probs)]
    wq = [jnp.concatenate([x[:, LANES:2 * LANES], qg[t_][sl]], axis=0).astype(BF16)
          for x, (sl, t_, _) in zip(uw, probs)]
    dg = [_dot(kgj, x, TN) for kgj, x in zip(kg, uw)]

    s = [s_scr[t_] for t_ in range(n_tiles)]
    for j, (sl, t_, ln) in enumerate(probs):
        ws = _dot(wq[j], s[t_])
        v_new = uw[j][:, 0:LANES] - ws[0:c]
        o_ref[0, sl, ln] = ws[c:2 * c] + _dot(qk[j], v_new)
        s[t_] = s[t_] * eg[j] - _dot(dg[j][:, LANES:2 * LANES], s[t_]) + dg[j][:, 0:LANES]
    for t_ in range(n_tiles):
        s_scr[t_] = s[t_]

    @pl.when(i == pl.num_programs(2) - 1)
    def _():
        for t_ in range(n_tiles):
            sfin_ref[0, t_] = s[t_]


def _gdn_scan(qkv, proj, d, reverse, avec, dvec, s0, *, b_width, tail_off):
    b, t, _ = qkv.shape
    nh = b_width // LANES
    g = _pick(nh, (SCAN_TILES, 1))
    gw = g * LANES
    tb = _pick(t, (256, 128, 64))
    nblk = t // tb
    hb = nh // g

    def tix(i):
        return (nblk - 1 - i) if reverse else i

    def seg_spec(k):
        return pl.BlockSpec((1, tb, gw), lambda bi, h, i: (bi, tix(i), k * hb + h))

    vec_spec = pl.BlockSpec((1, LANES), lambda bi, h, i: (0, 0))
    st_spec = pl.BlockSpec((1, g, LANES, LANES), lambda bi, h, i: (bi, h, 0, 0))
    in_specs = [seg_spec(0), seg_spec(1), seg_spec(2),
                pl.BlockSpec((1, tb, LANES), lambda bi, h, i: (bi, tix(i), tail_off // LANES)),
                vec_spec, vec_spec]
    args = [qkv, qkv, qkv, proj, avec, dvec]
    if s0 is not None:
        in_specs.append(st_spec)
        args.append(s0)
    return pl.pallas_call(
        functools.partial(_gdn_kernel, reverse=reverse, n_chunks=tb // CHUNK, zero_init=s0 is None, d=d, n_heads=nh,
                          n_tiles=g),
        out_shape=(jax.ShapeDtypeStruct((b, t, b_width), F32),
                   jax.ShapeDtypeStruct((b, nh, LANES, LANES), F32)),
        grid=(b, hb, nblk),
        in_specs=in_specs,
        out_specs=(pl.BlockSpec((1, tb, gw), lambda bi, h, i: (bi, tix(i), h)), st_spec),
        scratch_shapes=[pltpu.VMEM((g, LANES, LANES), F32)],
        compiler_params=_params(("parallel", "parallel", "arbitrary")),
        name="gdn_bwd" if reverse else "gdn_fwd",
    )(*args)


def _post_kernel(yf_ref, yb_ref, za_ref, of_ref, ob_ref, zb_ref, ga_ref, gb_ref, x_ref, mod_ref,
                 gnw_ref, gnb_ref, onw_ref, gpost_ref, wpa_ref, wpb_ref, wo_ref, o_ref, ya_scr, yb_scr, *, d):
    a_width = yf_ref.shape[-1]
    b_width = of_ref.shape[-1]
    row = lax.broadcasted_iota(jnp.int32, (LANES, LANES), 0)
    col = lax.broadcasted_iota(jnp.int32, (LANES, LANES), 1)
    seg_mean = ((row // A_HEAD_DIM) == (col // A_HEAD_DIM)).astype(F32) * (1.0 / A_HEAD_DIM)
    for p in range(a_width // LANES):
        sl = slice(p * LANES, (p + 1) * LANES)
        y = yf_ref[0, :, sl] + yb_ref[0, :, sl]
        cen = y - _dot(y, seg_mean, hi=True)
        var = _dot(cen * cen, seg_mean, hi=True)
        yn = cen * lax.rsqrt(var + RWKV_GN_EPS) * gnw_ref[:, sl] + gnb_ref[:, sl]
        ya_scr[:, sl] = (yn * _silu(za_ref[0, :, sl])).astype(BF16)
    for hh in range(b_width // LANES):
        sl = slice(hh * LANES, (hh + 1) * LANES)
        o = of_ref[0, :, sl] + ob_ref[0, :, sl]
        on = o * lax.rsqrt(jnp.mean(o * o, axis=-1, keepdims=True) + NORM_EPS) * onw_ref[:, sl]
        yb_scr[:, sl] = (on * _silu(zb_ref[0, :, sl])).astype(BF16)
    branch_a = jnp.dot(ya_scr[...], wpa_ref[...], preferred_element_type=F32)
    branch_b = jnp.dot(yb_scr[...], wpb_ref[...], preferred_element_type=F32)
    merged = _sigmoid(ga_ref[0]) * branch_a + _sigmoid(gb_ref[0]) * branch_b
    out = jnp.dot(merged.astype(BF16), wo_ref[...], preferred_element_type=F32)
    on = out * lax.rsqrt(jnp.mean(out * out, axis=-1, keepdims=True) + NORM_EPS) * gpost_ref[...]
    o_ref[0] = x_ref[0] + mod_ref[0, :, 2 * d:3 * d] * on


def _post(yf, yb, of, ob, proj, x, mod, gnw, gnb, onw, gpost, wpa, wpb, wo, *, a_width, b_width):
    b, t, d = x.shape
    tm = _pick(t, (256, 128, 64))
    za_blk = 3
    zb_blk = (4 * a_width + 3 * b_width) // b_width
    ga_blk = (4 * a_width + 4 * b_width) // d
    assert (4 * a_width + 3 * b_width) % b_width == 0 and (4 * a_width + 4 * b_width) % d == 0

    def row_spec(width, blk=0):
        return pl.BlockSpec((1, tm, width), lambda bi, i: (bi, i, blk))

    def full(shape):
        return pl.BlockSpec(shape, lambda bi, i: (0,) * len(shape), pipeline_mode=pl.Buffered(1))

    return pl.pallas_call(
        functools.partial(_post_kernel, d=d),
        out_shape=jax.ShapeDtypeStruct((b, t, d), F32),
        grid=(b, t // tm),
        in_specs=[row_spec(a_width), row_spec(a_width), row_spec(a_width, za_blk),
                  row_spec(b_width), row_spec(b_width), row_spec(b_width, zb_blk),
                  row_spec(d, ga_blk), row_spec(d, ga_blk + 1), row_spec(d),
                  pl.BlockSpec((1, 1, 3 * d), lambda bi, i: (bi, 0, 0)),
                  full((1, a_width)), full((1, a_width)), full((1, b_width)), full((1, d)),
                  full((a_width, d)), full((b_width, d)), full((d, d))],
        out_specs=row_spec(d),
        scratch_shapes=[pltpu.VMEM((tm, a_width), BF16), pltpu.VMEM((tm, b_width), BF16)],
        compiler_params=_params(("parallel", "parallel")),
        name="post",
    )(yf, yb, proj, of, ob, proj, proj, proj, x, mod, gnw, gnb, onw, gpost, wpa, wpb, wo)


def _grid_transpose(x, rows, cols):
    b, t, d = x.shape
    return x.reshape(b, rows, cols, d).swapaxes(1, 2).reshape(b, t, d)


def _pair_block_diag(s):
    b, h, n, _ = s.shape
    sp = s.reshape(b, h // 2, 2, n, n)
    eye2 = jnp.eye(2, dtype=s.dtype)
    return jnp.einsum('bpivk,ij->bpivjk', sp, eye2).reshape(b, h // 2, 2 * n, 2 * n)


def _pair_diag_blocks(s, n):
    b, p = s.shape[:2]
    s6 = s.reshape(b, p, 2, n, 2, n)
    return jnp.stack([s6[:, :, 0, :, 0, :], s6[:, :, 1, :, 1, :]], axis=2).reshape(b, 2 * p, n, n)


def _mixer_layer(x, mod, s_rwkv0, s_delta0, lp, dims):
    a_width, b_width = dims['a_width'], dims['b_width']
    proj = _in_proj(x, mod, lp['g_pre'], lp['w_in'])
    ys, srs, os_, sds = [], [], [], []
    qkv = _gdn_prep(proj, lp['conv_w'], b_width=b_width, qkv_off=4 * a_width)
    for d, rev in ((0, False), (1, True)):
        y, sr = _rwkv_scan(proj, d, rev, lp['wup_pad'], lp['aup_pad'], lp['w0'], lp['a0'], lp['k_k'], lp['k_a'],
                           lp['r_k'], None if s_rwkv0 is None else s_rwkv0[:, d],
                           a_width=a_width, lora_off=dims['lora_off'])
        o, sd = _gdn_scan(qkv, proj, d, rev, lp['avec'], lp['dvec'],
                          None if s_delta0 is None else s_delta0[:, d],
                          b_width=b_width, tail_off=dims['tail_off'])
        ys.append(y)
        srs.append(sr)
        os_.append(o)
        sds.append(sd)
    x_new = _post(ys[0], ys[1], os_[0], os_[1], proj, x, mod, lp['gn_w'], lp['gn_b'], lp['onw'], lp['g_post'],
                  lp['w_pa'], lp['w_pb'], lp['w_o'], a_width=a_width, b_width=b_width)
    return x_new, srs, sds


def kernel(x_prompt, x_sample, state_rwkv, state_delta, c, c_ctx, w_mod, b_mod, g_pre, g_post, w_in, w0, w_up, a0,
           a_up, k_k, k_a, r_k, gn_w, gn_b, conv_w, a_log, dt_bias, o_norm_w, w_pa, w_pb, w_o):
    bp, tp, dm = x_prompt.shape
    bs, ts, _ = x_sample.shape
    depth = w_mod.shape[0]
    a_width = k_k.shape[-1]
    n_bh = a_log.shape[-1]
    b_width = o_norm_w.shape[-1] * n_bh
    lora = w_up.shape[2]
    n_dir = 2
    rows = ts // GRID_W
    assert o_norm_w.shape[-1] == B_HEAD_DIM and a_width % LANES == 0 and 4 * n_bh <= LANES

    sizes = (a_width,) * 4 + (lora,) * 4 + (b_width,) * 4 + (n_dir * n_bh,) * 2 + (dm,) * 2
    offs = [0]
    for sz in sizes:
        offs.append(offs[-1] + sz)
    lora_w = 4 * lora
    lora_off = 4 * a_width + 4 * b_width + 2 * dm
    tail_off = lora_off + lora_w
    tail_used = 2 * n_dir * n_bh
    assert lora_off % lora_w == 0 and tail_off % LANES == 0
    dims = dict(a_width=a_width, b_width=b_width, lora_off=lora_off, tail_off=tail_off)

    def pack_w_in(w):
        pad = jnp.zeros((w.shape[0], LANES - tail_used), w.dtype)
        return jnp.concatenate([w[:, offs[0]:offs[4]], w[:, offs[8]:offs[12]], w[:, offs[14]:offs[16]],
                                w[:, offs[4]:offs[8]], w[:, offs[12]:offs[14]], pad], axis=1).astype(BF16)

    def pad_up(w, first):
        z = jnp.zeros((n_dir, lora_w, a_width), w.dtype)
        for d in range(n_dir):
            z = z.at[d, (first + d) * lora:(first + d + 1) * lora].set(w[d])
        return z.astype(BF16)

    def tail_vec(v):
        z = jnp.zeros((1, LANES), F32)
        return z.at[0, n_dir * n_bh:2 * n_dir * n_bh].set(v.reshape(-1))

    cvec = jnp.concatenate([c_ctx[None], c, jnp.zeros((8 - 1 - bs, dm), F32)], axis=0)
    mods = _modulation(cvec, w_mod, b_mod)

    xp = x_prompt.reshape(1, bp * tp, dm)
    xs = x_sample
    new_r, new_d = [], []
    for l in range(depth):
        lp = dict(
            g_pre=g_pre[l][None], g_post=g_post[l][None], w_in=pack_w_in(w_in[l]),
            wup_pad=pad_up(w_up[l], 0), aup_pad=pad_up(a_up[l], 2),
            w0=w0[l][:, None], a0=a0[l][:, None], k_k=k_k[l][None], k_a=k_a[l][None], r_k=r_k[l][None],
            gn_w=gn_w[l][None], gn_b=gn_b[l][None], conv_w=conv_w[l],
            avec=tail_vec(a_log[l]), dvec=tail_vec(dt_bias[l]),
            onw=jnp.tile(o_norm_w[l], n_bh)[None],
            w_pa=w_pa[l].astype(BF16), w_pb=w_pb[l].astype(BF16), w_o=w_o[l].astype(BF16))
        mod_ctx = mods[l, 0:1][None]
        xp3 = xp.reshape(bp, tp, dm)
        xp_new, srs, sds = _mixer_layer_ctx(xp3, mod_ctx, lp, dims)
        xp = xp_new
        new_r.append(jnp.stack([_pair_diag_blocks(s, A_HEAD_DIM) for s in srs], axis=1))
        new_d.append(jnp.stack(sds, axis=1))
        mod_lat = mods[l, 1:1 + bs][:, None]
        s_r0 = jnp.stack([_pair_block_diag(state_rwkv[:, l, d]) for d in range(n_dir)], axis=1)
        s_d0 = state_delta[:, l]
        if l % 2 == 1:
            xs = _grid_transpose(xs, rows, GRID_W)
        xs, _, _ = _mixer_layer(xs, mod_lat, s_r0, s_d0, lp, dims)
        if l % 2 == 1:
            xs = _grid_transpose(xs, GRID_W, rows)
    y_prompt = xp.reshape(bp, tp, dm)
    new_state_rwkv = jnp.stack(new_r, axis=1)
    new_state_delta = jnp.stack(new_d, axis=1)
    return (y_prompt, xs, new_state_rwkv, new_state_delta)


def _mixer_layer_ctx(x, mod, lp, dims):
    b, t, d = x.shape
    a_width, b_width = dims['a_width'], dims['b_width']
    proj = _in_proj(x.reshape(1, b * t, d), mod, lp['g_pre'], lp['w_in']).reshape(b, t, -1)
    qkv = _gdn_prep(proj, lp['conv_w'], b_width=b_width, qkv_off=4 * a_width)
    ys, srs, os_, sds = [], [], [], []
    for dd, rev in ((0, False), (1, True)):
        y, sr = _rwkv_scan(proj, dd, rev, lp['wup_pad'], lp['aup_pad'], lp['w0'], lp['a0'], lp['k_k'], lp['k_a'],
                           lp['r_k'], None, a_width=a_width, lora_off=dims['lora_off'])
        o, sd = _gdn_scan(qkv, proj, dd, rev, lp['avec'], lp['dvec'], None,
                          b_width=b_width, tail_off=dims['tail_off'])
        ys.append(y)
        srs.append(sr)
        os_.append(o)
        sds.append(sd)
    flat = lambda z: z.reshape(1, b * t, z.shape[-1])
    x_new = _post(flat(ys[0]), flat(ys[1]), flat(os_[0]), flat(os_[1]), flat(proj), flat(x), mod,
                  lp['gn_w'], lp['gn_b'], lp['onw'], lp['g_post'], lp['w_pa'], lp['w_pb'], lp['w_o'],
                  a_width=a_width, b_width=b_width)
    return x_new.reshape(b, t, d), srs, sds
```

```python
import functools

import jax
import jax.numpy as jnp
from jax import lax
from jax.experimental import pallas as pl
from jax.experimental.pallas import tpu as pltpu

F32 = jnp.float32
BF16 = jnp.bfloat16

GRID_W = 64
A_HEAD_DIM = 64
B_HEAD_DIM = 128
CHUNK = 64
SCAN_TILES = 8
TRI_BASE = 16
DECAY_SCALE = 0.606531
RWKV_GN_EPS = 64e-5
NORM_EPS = 1e-6
LANES = 128
MXU_WIDTH = 256
VMEM_LIMIT = 56 * 1024 * 1024
PREP_BLOCK_BYTES = 2 * 1024 * 1024

NN = (((1,), (0,)), ((), ()))
NT = (((1,), (1,)), ((), ()))
TN = (((0,), (0,)), ((), ()))


def _dot(a, b, dims=NN, hi=False):
    if hi:
        return lax.dot_general(a, b, dims, precision=lax.Precision.HIGHEST, preferred_element_type=F32)
    return lax.dot_general(a.astype(BF16), b.astype(BF16), dims, preferred_element_type=F32)


def _sigmoid(x):
    return jax.nn.sigmoid(x)


def _silu(x):
    return x * jax.nn.sigmoid(x)


def _params(sem):
    return pltpu.CompilerParams(dimension_semantics=sem, vmem_limit_bytes=VMEM_LIMIT)


def _pick(n, cands):
    for c in cands:
        if n % c == 0:
            return c
    raise ValueError(f"no tile for {n}")


def _mod_kernel(c_ref, w_ref, b_ref, o_ref):
    o_ref[0] = _dot(_silu(c_ref[...]), w_ref[0]) + b_ref[0]


def _modulation(cvec, w_mod, b_mod):
    depth, d, n = w_mod.shape
    tn = _pick(n, (512, 256, 128))
    return pl.pallas_call(
        _mod_kernel,
        out_shape=jax.ShapeDtypeStruct((depth, 8, n), F32),
        grid=(depth, n // tn),
        in_specs=[pl.BlockSpec((8, d), lambda l, j: (0, 0)),
                  pl.BlockSpec((1, d, tn), lambda l, j: (l, 0, j)),
                  pl.BlockSpec((1, 1, tn), lambda l, j: (l, 0, j))],
        out_specs=pl.BlockSpec((1, 8, tn), lambda l, j: (l, 0, j)),
        compiler_params=_params(("parallel", "parallel")),
        name="modulation",
    )(cvec, w_mod, b_mod.reshape(depth, 1, n))


def _inproj_kernel(x_ref, mod_ref, g_ref, w_ref, o_ref, h_scr, *, d):
    @pl.when(pl.program_id(2) == 0)
    def _():
        x = x_ref[0]
        y = x * lax.rsqrt(jnp.mean(x * x, axis=-1, keepdims=True) + NORM_EPS) * g_ref[...]
        shift = mod_ref[0, :, 0:d]
        scale = mod_ref[0, :, d:2 * d]
        h_scr[...] = (y * (1.0 + scale) + shift).astype(BF16)

    o_ref[0] = jnp.dot(h_scr[...], w_ref[...], preferred_element_type=F32)


def _in_proj(x, mod, g_pre, w_packed):
    b, t, d = x.shape
    n = w_packed.shape[1]
    tm = _pick(t, (1024, 512, 256, 128))
    tn = _pick(n, (1280, 1024, 512, 256, 128))
    return pl.pallas_call(
        functools.partial(_inproj_kernel, d=d),
        out_shape=jax.ShapeDtypeStruct((b, t, n), F32),
        grid=(b, t // tm, n // tn),
        in_specs=[pl.BlockSpec((1, tm, d), lambda bi, i, j: (bi, i, 0)),
                  pl.BlockSpec((1, 1, 3 * d), lambda bi, i, j: (bi, 0, 0)),
                  pl.BlockSpec((1, d), lambda bi, i, j: (0, 0)),
                  pl.BlockSpec((d, tn), lambda bi, i, j: (0, j))],
        out_specs=pl.BlockSpec((1, tm, tn), lambda bi, i, j: (bi, i, j)),
        scratch_shapes=[pltpu.VMEM((tm, d), BF16)],
        compiler_params=_params(("parallel", "parallel", "arbitrary")),
        name="in_proj",
    )(x, mod, g_pre, w_packed)


def _tri_masks(n, period, reverse):
    ti = lax.broadcasted_iota(jnp.int32, (n, n), 0) & (period - 1)
    sj = lax.broadcasted_iota(jnp.int32, (n, n), 1) & (period - 1)
    if reverse:
        return sj > ti, sj >= ti
    return sj < ti, sj <= ti


def _unit_tri_inverse(negs, eye):
    n = eye.shape[0]
    row = lax.broadcasted_iota(jnp.int32, (n, n), 0)
    col = lax.broadcasted_iota(jnp.int32, (n, n), 1)

    def same_block(size):
        return (row // size) == (col // size)

    base_mask = same_block(TRI_BASE)
    ps = [jnp.where(base_mask, na, 0.0) for na in negs]
    ts = [eye + p for p in ps]
    steps = TRI_BASE.bit_length() - 2
    ps = [_dot(p, p) for p in ps]
    for i in range(steps):
        if i + 1 < steps:
            tps = [_dot(jnp.concatenate([t, p], axis=0), p) for t, p in zip(ts, ps)]
            ts = [t + tp[0:n] for t, tp in zip(ts, tps)]
            ps = [tp[n:2 * n] for tp in tps]
        else:
            ts = [t + _dot(t, p) for t, p in zip(ts, ps)]
    size = TRI_BASE
    while size < CHUNK:
        cmask = jnp.logical_and(same_block(2 * size), jnp.logical_not(same_block(size)))
        ms = [_dot(t, jnp.where(cmask, na, 0.0)) for t, na in zip(ts, negs)]
        ts = [t + _dot(m, t) for t, m in zip(ts, ms)]
        size *= 2
    return ts


def _masked_cumsum(mask01, x):
    n = x.shape[1]
    hi = x.astype(BF16)
    lo = (x - hi.astype(F32)).astype(BF16)
    both = lax.dot_general(mask01.astype(BF16), jnp.concatenate([hi, lo], axis=1), NN, preferred_element_type=F32)
    return both[:, 0:n] + both[:, n:2 * n]


def _rwkv_kernel(*refs, reverse, n_chunks, zero_init, n_tiles, lora_split):
    (r_ref, k_ref, v_ref, lora_ref, wup_ref, aup_ref, w0_ref, a0_ref, kkw_ref, kaw_ref, rkw_ref) = refs[:11]
    if zero_init:
        _, y_ref, sfin_ref, s_scr = refs[11:]
        s0_ref = None
    else:
        s0_ref, y_ref, s_scr = refs[11:]
    i = pl.program_id(2)
    c = CHUNK
    n2 = 2 * c
    width = n_tiles * LANES

    @pl.when(i == 0)
    def _():
        if zero_init:
            s_scr[...] = jnp.zeros_like(s_scr)
        else:
            s_scr[...] = s0_ref[0]

    strict, incl = _tri_masks(n2, c, reverse)
    row = lax.broadcasted_iota(jnp.int32, (n2, n2), 0)
    col = lax.broadcasted_iota(jnp.int32, (n2, n2), 1)
    eye = (row == col).astype(F32)
    sw = min(width, MXU_WIDTH)
    wrow = lax.broadcasted_iota(jnp.int32, (sw, sw), 0)
    wcol = lax.broadcasted_iota(jnp.int32, (sw, sw), 1)
    seg = ((wrow // A_HEAD_DIM) == (wcol // A_HEAD_DIM)).astype(BF16)
    head0 = lax.broadcasted_iota(jnp.int32, (c, LANES), 1) < A_HEAD_DIM

    def stack(z):
        return jnp.concatenate([jnp.where(head0, z, 0.0), jnp.where(head0, 0.0, z)], axis=0).astype(BF16)

    r_all = r_ref[0]
    k_all = k_ref[0]
    v_all = v_ref[0]
    lo = lora_ref[0]
    tb = r_all.shape[0]
    w_hi = -(-lora_split // LANES) * LANES
    a_lo = (lora_split // LANES) * LANES
    logw = -DECAY_SCALE * _sigmoid(w0_ref[0] + _dot(jnp.tanh(lo[:, 0:w_hi]), wup_ref[0, 0:w_hi, :]))
    a = _sigmoid(a0_ref[0] + _dot(lo[:, a_lo:], aup_ref[0, a_lo:, :]))
    kx = k_all * kkw_ref[...]
    kd = k_all * (1.0 + (a - 1.0) * kaw_ref[...])
    stat_in = jnp.concatenate([kx * kx, r_all * kd * rkw_ref[...]], axis=0).astype(BF16)
    sums = jnp.concatenate([_dot(stat_in[:, q * sw:(q + 1) * sw], seg) for q in range(width // sw)], axis=1)
    kk = kx * lax.rsqrt(sums[0:tb] + NORM_EPS)
    bonus = sums[tb:2 * tb] * v_all
    bb = kk * a
    trow = lax.broadcasted_iota(jnp.int32, (tb, tb), 0)
    tcol = lax.broadcasted_iota(jnp.int32, (tb, tb), 1)
    same_chunk = (trow // c) == (tcol // c)
    cmask = jnp.logical_and(same_chunk, (tcol >= trow) if reverse else (tcol <= trow))
    cs = _masked_cumsum(cmask, logw)
    einv = jnp.exp(-cs)
    rt = r_all * jnp.exp(cs)
    kkt = kk * jnp.exp(cs - logw)
    bt = bb * einv
    kt = kd * einv

    chunks = list(range(n_chunks - 1, -1, -1) if reverse else range(n_chunks))
    probs = [(slice(ci * c, (ci + 1) * c), g, slice(g * LANES, (g + 1) * LANES))
             for ci in chunks for g in range(n_tiles)]
    rt_s = [stack(rt[sl, ln]) for sl, _, ln in probs]
    kkt_s = [stack(kkt[sl, ln]) for sl, _, ln in probs]
    bt_s = [stack(bt[sl, ln]) for sl, _, ln in probs]
    kt_s = [stack(kt[sl, ln]) for sl, _, ln in probs]
    v_s = [stack(v_all[sl, ln]) for sl, _, ln in probs]
    ptot = [jnp.exp(jnp.sum(logw[sl, ln], axis=0, keepdims=True)) for sl, _, ln in probs]

    big = [_dot(jnp.concatenate([x, y], axis=0), jnp.concatenate([z, w], axis=0), NT)
           for x, y, z, w in zip(kkt_s, rt_s, bt_s, kt_s)]
    tinv = _unit_tri_inverse([jnp.where(strict, -g[0:n2, 0:n2], 0.0) for g in big], eye)
    av = [_dot(jnp.concatenate([jnp.where(strict, g[0:n2, n2:2 * n2], 0.0),
                                jnp.where(incl, g[n2:2 * n2, n2:2 * n2], 0.0)], axis=0), vs)
          for g, vs in zip(big, v_s)]
    rbm = [jnp.where(incl, g[n2:2 * n2, 0:n2], 0.0).astype(BF16) for g in big]
    wu = [_dot(t, jnp.concatenate([x, y[0:n2].astype(BF16)], axis=1)) for t, x, y in zip(tinv, kkt_s, av)]
    wr = [jnp.concatenate([x[:, 0:n2].astype(BF16), y], axis=0) for x, y in zip(wu, rt_s)]
    u0 = [x[:, n2:2 * n2] for x in wu]
    o0 = [y[n2:2 * n2] for y in av]
    gd = [_dot(jnp.concatenate([x[:, 0:n2], -x[:, n2:2 * n2]], axis=1), z, TN) for x, z in zip(wu, bt_s)]
    dmat = [_dot(vs, ks, TN) + g[n2:2 * n2] for vs, ks, g in zip(v_s, kt_s, gd)]
    gmat = [g[0:n2].astype(BF16) for g in gd]

    s = [s_scr[g] for g in range(n_tiles)]
    for j, (sl, g, ln) in enumerate(probs):
        xr = _dot(wr[j], s[g], NT)
        u = -(xr[0:n2] + u0[j])
        o_st = xr[n2:2 * n2] + _dot(rbm[j], u) + o0[j]
        y_ref[0, sl, ln] = o_st[0:c] + o_st[c:n2] + bonus[sl, ln]
        s[g] = (s[g] - _dot(s[g], gmat[j]) + dmat[j]) * ptot[j]
    for g in range(n_tiles):
        s_scr[g] = s[g]

    if zero_init:
        @pl.when(i == pl.num_programs(2) - 1)
        def _():
            hd = A_HEAD_DIM
            for g in range(n_tiles):
                sfin_ref[0, 0, 0, 2 * g] = s[g][0:hd, 0:hd]
                sfin_ref[0, 0, 0, 2 * g + 1] = s[g][hd:2 * hd, hd:2 * hd]


def _rwkv_scan(proj, d, reverse, wup_pad, aup_pad, w0, a0, kkw, kaw, rkw, s0, state_acc, layer, *, a_width, lora_off):
    b, t, _ = proj.shape
    n_pairs = a_width // LANES
    g = _pick(n_pairs, (SCAN_TILES, 1))
    gw = g * LANES
    tb = _pick(t, (256, 128, 64))
    nblk = t // tb
    lw = wup_pad.shape[1]
    ab = a_width // gw

    def tix(i):
        return (nblk - 1 - i) if reverse else i

    def seg_spec(k):
        return pl.BlockSpec((1, tb, gw), lambda bi, p, i: (bi, tix(i), k * ab + p))

    vec_spec = pl.BlockSpec((1, gw), lambda bi, p, i: (0, p))
    dvec_spec = pl.BlockSpec((1, 1, gw), lambda bi, p, i: (d, 0, p))
    up_spec = pl.BlockSpec((1, lw, gw), lambda bi, p, i: (d, 0, p))
    st_spec = pl.BlockSpec((1, g, LANES, LANES), lambda bi, p, i: (bi, p, 0, 0))
    in_specs = [seg_spec(0), seg_spec(1), seg_spec(2),
                pl.BlockSpec((1, tb, lw), lambda bi, p, i: (bi, tix(i), lora_off // lw)),
                up_spec, up_spec, dvec_spec, dvec_spec, vec_spec, vec_spec, vec_spec]
    args = [proj, proj, proj, proj, wup_pad, aup_pad, w0, a0, kkw, kaw, rkw]
    y_shape = jax.ShapeDtypeStruct((b, t, a_width), F32)
    y_spec = pl.BlockSpec((1, tb, gw), lambda bi, p, i: (bi, tix(i), p))
    if s0 is not None:
        in_specs.append(st_spec)
        args.append(s0)
        out_shape, out_specs, aliases = y_shape, y_spec, {}
    else:
        in_specs.append(pl.BlockSpec(memory_space=pl.ANY))
        args.append(state_acc)
        hd = A_HEAD_DIM
        acc_spec = pl.BlockSpec((1, 1, 1, 2 * g, hd, hd), lambda bi, p, i: (bi, layer, d, p, 0, 0))
        out_shape = (y_shape, jax.ShapeDtypeStruct(state_acc.shape, state_acc.dtype))
        out_specs = (y_spec, acc_spec)
        aliases = {len(args) - 1: 1}
    out = pl.pallas_call(
        functools.partial(_rwkv_kernel, reverse=reverse, n_chunks=tb // CHUNK, zero_init=s0 is None, n_tiles=g,
                          lora_split=lw // 2),
        out_shape=out_shape,
        grid=(b, n_pairs // g, nblk),
        in_specs=in_specs,
        out_specs=out_specs,
        input_output_aliases=aliases,
        scratch_shapes=[pltpu.VMEM((g, LANES, LANES), F32)],
        compiler_params=_params(("parallel", "parallel", "arbitrary")),
        name="rwkv_bwd" if reverse else "rwkv_fwd",
    )(*args)
    return (out, None) if s0 is not None else out


def _gdn_prep_kernel(x_ref, cw_ref, o_ref, *, blocks_per_part):
    part = pl.program_id(1) // blocks_per_part
    t = x_ref.shape[1]
    rowi = lax.broadcasted_iota(jnp.int32, (t, LANES), 0)
    first = rowi == 0
    last = rowi == t - 1
    q_scale = jnp.where(part == 0, B_HEAD_DIM ** -0.5, 1.0)
    for k in range(x_ref.shape[2] // LANES):
        sl = slice(k * LANES, (k + 1) * LANES)
        x = x_ref[0, :, sl]
        xm = jnp.where(first, 0.0, pltpu.roll(x, 1, 0))
        xp = jnp.where(last, 0.0, pltpu.roll(x, t - 1, 0))
        y = _silu(xm * cw_ref[0:1, sl] + x * cw_ref[1:2, sl] + xp * cw_ref[2:3, sl])
        nrm = y * (lax.rsqrt(jnp.sum(y * y, axis=-1, keepdims=True) + NORM_EPS) * q_scale)
        o_ref[0, :, sl] = jnp.where(part < 2, nrm, y)


def _gdn_prep(proj, conv_w, *, b_width, qkv_off):
    b, t, _ = proj.shape
    cw = _pick(b_width, tuple(w for w in (1024, 512, 256, 128) if w * t * 4 <= PREP_BLOCK_BYTES) + (LANES,))
    nblk = b_width // cw
    return pl.pallas_call(
        functools.partial(_gdn_prep_kernel, blocks_per_part=nblk),
        out_shape=jax.ShapeDtypeStruct((b, t, 3 * b_width), F32),
        grid=(b, 3 * nblk),
        in_specs=[pl.BlockSpec((1, t, cw), lambda bi, j: (bi, 0, qkv_off // cw + j)),
                  pl.BlockSpec((3, cw), lambda bi, j: (0, j))],
        out_specs=pl.BlockSpec((1, t, cw), lambda bi, j: (bi, 0, j)),
        compiler_params=_params(("parallel", "parallel")),
        name="gdn_prep",
    )(proj, conv_w)


def _softplus(x):
    return jnp.maximum(x, 0.0) + jnp.log1p(jnp.exp(-jnp.abs(x)))


def _gdn_kernel(*refs, reverse, n_chunks, zero_init, d, n_heads, n_tiles):
    q_ref, k_ref, v_ref, tail_ref, avec_ref, dvec_ref = refs[:6]
    if zero_init:
        _, o_ref, sfin_ref, s_scr = refs[6:]
        s0_ref = None
    else:
        s0_ref, o_ref, s_scr = refs[6:]
    h0 = pl.program_id(1) * n_tiles
    i = pl.program_id(2)
    c = CHUNK

    @pl.when(i == 0)
    def _():
        if zero_init:
            s_scr[...] = jnp.zeros_like(s_scr)
        else:
            s_scr[...] = s0_ref[0]

    strict, incl = _tri_masks(c, c, reverse)
    row = lax.broadcasted_iota(jnp.int32, (c, c), 0)
    col = lax.broadcasted_iota(jnp.int32, (c, c), 1)
    eye = (row == col).astype(F32)

    tail = tail_ref[0]
    tb = tail.shape[0]
    lane = lax.broadcasted_iota(jnp.int32, (tb, LANES), 1)
    beta_all = _sigmoid(tail)
    g_all = -jnp.exp(avec_ref[...]) * _softplus(tail + dvec_ref[...])
    pos = lax.broadcasted_iota(jnp.int32, (tb, LANES), 0) & (c - 1)
    keep = jnp.logical_or(lane >= c, (pos < lane) if reverse else (pos > lane))
    trow = lax.broadcasted_iota(jnp.int32, (tb, tb), 0)
    tcol = lax.broadcasted_iota(jnp.int32, (tb, tb), 1)
    same_chunk = (trow // c) == (tcol // c)
    cmask = jnp.logical_and(same_chunk, (tcol >= trow) if reverse else (tcol <= trow))
    beta, gcol, gpack = [], [], []
    for t_ in range(n_tiles):
        h = h0 + t_
        beta.append(jnp.sum(jnp.where(lane == d * n_heads + h, beta_all, 0.0), axis=-1, keepdims=True))
        gcol.append(jnp.sum(jnp.where(lane == (2 + d) * n_heads + h, g_all, 0.0), axis=-1, keepdims=True))
        gpack.append(jnp.where(keep, jnp.broadcast_to(gcol[t_], (tb, LANES)), 0.0))
    cum = _masked_cumsum(cmask, jnp.concatenate(gpack, axis=1))
    q_all = q_ref[0]
    k_all = k_ref[0]
    v_all = v_ref[0]
    diff, gcum, kb, vb_kbe, qg = [], [], [], [], []
    for t_ in range(n_tiles):
        ln = slice(t_ * LANES, (t_ + 1) * LANES)
        diff.append(cum[:, t_ * LANES:t_ * LANES + c])
        gcum.append(jnp.broadcast_to(cum[:, t_ * LANES + c:t_ * LANES + c + 1], (tb, LANES)))
        egc = jnp.exp(gcum[t_])
        kb.append(k_all[:, ln] * beta[t_])
        vb_kbe.append(jnp.concatenate([v_all[:, ln] * beta[t_], kb[t_] * egc], axis=1))
        qg.append(q_all[:, ln] * egc)

    chunks = list(range(n_chunks - 1, -1, -1) if reverse else range(n_chunks))
    probs = [(slice(ci * c, (ci + 1) * c), t_, slice(t_ * LANES, (t_ + 1) * LANES))
             for ci in chunks for t_ in range(n_tiles)]
    glast = [jnp.sum(jnp.broadcast_to(gcol[t_][sl], (c, LANES)), axis=0, keepdims=True) for sl, t_, _ in probs]
    kg = [(k_all[sl, ln] * jnp.exp(gl - gcum[t_][sl])).astype(BF16) for (sl, t_, ln), gl in zip(probs, glast)]
    eg = [jnp.exp(gl) for gl in glast]
    decay = [jnp.where(incl, jnp.exp(diff[t_][sl]), 0.0) for sl, t_, _ in probs]

    qkk = [_dot(jnp.concatenate([kb[t_][sl], q_all[sl, ln]], axis=0), k_all[sl, ln], NT) for sl, t_, ln in probs]
    tinv = _unit_tri_inverse([jnp.where(strict, -(x[0:c] * dc), 0.0) for x, dc in zip(qkk, decay)], eye)
    qk = [(x[c:2 * c] * dc).astype(BF16) for x, dc in zip(qkk, decay)]
    uw = [_dot(t, vb_kbe[t_][sl]) for t, (sl, t_, _) in zip(tinv, probs)]
    wq = [jnp.concatenate([x[:, LANES:2 * LANES], qg[t_][sl]], axis=0).astype(BF16)
          for x, (sl, t_, _) in zip(uw, probs)]
    dg = [_dot(kgj, x, TN) for kgj, x in zip(kg, uw)]

    s = [s_scr[t_] for t_ in range(n_tiles)]
    for j, (sl, t_, ln) in enumerate(probs):
        ws = _dot(wq[j], s[t_])
        v_new = uw[j][:, 0:LANES] - ws[0:c]
        o_ref[0, sl, ln] = ws[c:2 * c] + _dot(qk[j], v_new)
        s[t_] = s[t_] * eg[j] - _dot(dg[j][:, LANES:2 * LANES], s[t_]) + dg[j][:, 0:LANES]
    for t_ in range(n_tiles):
        s_scr[t_] = s[t_]

    if zero_init:
        @pl.when(i == pl.num_programs(2) - 1)
        def _():
            for t_ in range(n_tiles):
                sfin_ref[0, 0, 0, t_] = s[t_]


def _gdn_scan(qkv, proj, d, reverse, avec, dvec, s0, state_acc, layer, *, b_width, tail_off):
    b, t, _ = qkv.shape
    nh = b_width // LANES
    g = _pick(nh, (SCAN_TILES, 1))
    gw = g * LANES
    tb = _pick(t, (256, 128, 64))
    nblk = t // tb
    hb = nh // g

    def tix(i):
        return (nblk - 1 - i) if reverse else i

    def seg_spec(k):
        return pl.BlockSpec((1, tb, gw), lambda bi, h, i: (bi, tix(i), k * hb + h))

    vec_spec = pl.BlockSpec((1, LANES), lambda bi, h, i: (0, 0))
    st_spec = pl.BlockSpec((1, g, LANES, LANES), lambda bi, h, i: (bi, h, 0, 0))
    in_specs = [seg_spec(0), seg_spec(1), seg_spec(2),
                pl.BlockSpec((1, tb, LANES), lambda bi, h, i: (bi, tix(i), tail_off // LANES)),
                vec_spec, vec_spec]
    args = [qkv, qkv, qkv, proj, avec, dvec]
    o_shape = jax.ShapeDtypeStruct((b, t, b_width), F32)
    o_spec = pl.BlockSpec((1, tb, gw), lambda bi, h, i: (bi, tix(i), h))
    if s0 is not None:
        in_specs.append(st_spec)
        args.append(s0)
        out_shape, out_specs, aliases = o_shape, o_spec, {}
    else:
        in_specs.append(pl.BlockSpec(memory_space=pl.ANY))
        args.append(state_acc)
        acc_spec = pl.BlockSpec((1, 1, 1, g, LANES, LANES), lambda bi, h, i: (bi, layer, d, h, 0, 0))
        out_shape = (o_shape, jax.ShapeDtypeStruct(state_acc.shape, state_acc.dtype))
        out_specs = (o_spec, acc_spec)
        aliases = {len(args) - 1: 1}
    out = pl.pallas_call(
        functools.partial(_gdn_kernel, reverse=reverse, n_chunks=tb // CHUNK, zero_init=s0 is None, d=d, n_heads=nh,
                          n_tiles=g),
        out_shape=out_shape,
        grid=(b, hb, nblk),
        in_specs=in_specs,
        out_specs=out_specs,
        input_output_aliases=aliases,
        scratch_shapes=[pltpu.VMEM((g, LANES, LANES), F32)],
        compiler_params=_params(("parallel", "parallel", "arbitrary")),
        name="gdn_bwd" if reverse else "gdn_fwd",
    )(*args)
    return (out, None) if s0 is not None else out


def _post_kernel(yf_ref, yb_ref, za_ref, of_ref, ob_ref, zb_ref, ga_ref, gb_ref, x_ref, mod_ref,
                 gnw_ref, gnb_ref, onw_ref, gpost_ref, wpa_ref, wpb_ref, wo_ref, o_ref, ya_scr, yb_scr, *, d):
    a_width = yf_ref.shape[-1]
    b_width = of_ref.shape[-1]
    row = lax.broadcasted_iota(jnp.int32, (LANES, LANES), 0)
    col = lax.broadcasted_iota(jnp.int32, (LANES, LANES), 1)
    seg_mean = (((row // A_HEAD_DIM) == (col // A_HEAD_DIM)).astype(F32) * (1.0 / A_HEAD_DIM)).astype(BF16)
    tm = yf_ref.shape[1]
    for p in range(a_width // LANES):
        sl = slice(p * LANES, (p + 1) * LANES)
        y = yf_ref[0, :, sl] + yb_ref[0, :, sl]
        y_hi = y.astype(BF16)
        y_lo = (y - y_hi.astype(F32)).astype(BF16)
        mean2 = _dot(jnp.concatenate([y_hi, y_lo], axis=0), seg_mean)
        cen = y - (mean2[0:tm] + mean2[tm:2 * tm])
        var = _dot(cen * cen, seg_mean)
        yn = cen * lax.rsqrt(var + RWKV_GN_EPS) * gnw_ref[:, sl] + gnb_ref[:, sl]
        ya_scr[:, sl] = (yn * _silu(za_ref[0, :, sl])).astype(BF16)
    for hh in range(b_width // LANES):
        sl = slice(hh * LANES, (hh + 1) * LANES)
        o = of_ref[0, :, sl] + ob_ref[0, :, sl]
        on = o * lax.rsqrt(jnp.mean(o * o, axis=-1, keepdims=True) + NORM_EPS) * onw_ref[:, sl]
        yb_scr[:, sl] = (on * _silu(zb_ref[0, :, sl])).astype(BF16)
    branch_a = jnp.dot(ya_scr[...], wpa_ref[...], preferred_element_type=F32)
    branch_b = jnp.dot(yb_scr[...], wpb_ref[...], preferred_element_type=F32)
    merged = _sigmoid(ga_ref[0]) * branch_a + _sigmoid(gb_ref[0]) * branch_b
    out = jnp.dot(merged.astype(BF16), wo_ref[...], preferred_element_type=F32)
    on = out * lax.rsqrt(jnp.mean(out * out, axis=-1, keepdims=True) + NORM_EPS) * gpost_ref[...]
    o_ref[0] = x_ref[0] + mod_ref[0, :, 2 * d:3 * d] * on


def _post(yf, yb, of, ob, proj, x, mod, gnw, gnb, onw, gpost, wpa, wpb, wo, *, a_width, b_width):
    b, t, d = x.shape
    tm = _pick(t, (256, 128, 64))
    za_blk = 3
    zb_blk = (4 * a_width + 3 * b_width) // b_width
    ga_blk = (4 * a_width + 4 * b_width) // d
    assert (4 * a_width + 3 * b_width) % b_width == 0 and (4 * a_width + 4 * b_width) % d == 0

    def row_spec(width, blk=0):
        return pl.BlockSpec((1, tm, width), lambda bi, i: (bi, i, blk))

    def full(shape):
        return pl.BlockSpec(shape, lambda bi, i: (0,) * len(shape), pipeline_mode=pl.Buffered(1))

    return pl.pallas_call(
        functools.partial(_post_kernel, d=d),
        out_shape=jax.ShapeDtypeStruct((b, t, d), F32),
        grid=(b, t // tm),
        in_specs=[row_spec(a_width), row_spec(a_width), row_spec(a_width, za_blk),
                  row_spec(b_width), row_spec(b_width), row_spec(b_width, zb_blk),
                  row_spec(d, ga_blk), row_spec(d, ga_blk + 1), row_spec(d),
                  pl.BlockSpec((1, 1, 3 * d), lambda bi, i: (bi, 0, 0)),
                  full((1, a_width)), full((1, a_width)), full((1, b_width)), full((1, d)),
                  full((a_width, d)), full((b_width, d)), full((d, d))],
        out_specs=row_spec(d),
        scratch_shapes=[pltpu.VMEM((tm, a_width), BF16), pltpu.VMEM((tm, b_width), BF16)],
        compiler_params=_params(("parallel", "parallel")),
        name="post",
    )(yf, yb, proj, of, ob, proj, proj, proj, x, mod, gnw, gnb, onw, gpost, wpa, wpb, wo)


def _grid_transpose(x, rows, cols):
    b, t, d = x.shape
    return x.reshape(b, rows, cols, d).swapaxes(1, 2).reshape(b, t, d)


def _pair_block_diag(s):
    b, h, n, _ = s.shape
    sp = s.reshape(b, h // 2, 2, n, n)
    eye2 = jnp.eye(2, dtype=s.dtype)
    return jnp.einsum('bpivk,ij->bpivjk', sp, eye2).reshape(b, h // 2, 2 * n, 2 * n)


def _mixer_layer(x, mod, s_rwkv0, s_delta0, lp, dims):
    a_width, b_width = dims['a_width'], dims['b_width']
    proj = _in_proj(x, mod, lp['g_pre'], lp['w_in'])
    ys, os_ = [], []
    qkv = _gdn_prep(proj, lp['conv_w'], b_width=b_width, qkv_off=4 * a_width)
    for d, rev in ((0, False), (1, True)):
        y, _ = _rwkv_scan(proj, d, rev, lp['wup_pad'], lp['aup_pad'], lp['w0'], lp['a0'], lp['k_k'], lp['k_a'],
                          lp['r_k'], s_rwkv0[:, d], None, None, a_width=a_width, lora_off=dims['lora_off'])
        o, _ = _gdn_scan(qkv, proj, d, rev, lp['avec'], lp['dvec'], s_delta0[:, d], None, None,
                         b_width=b_width, tail_off=dims['tail_off'])
        ys.append(y)
        os_.append(o)
    return _post(ys[0], ys[1], os_[0], os_[1], proj, x, mod, lp['gn_w'], lp['gn_b'], lp['onw'], lp['g_post'],
                 lp['w_pa'], lp['w_pb'], lp['w_o'], a_width=a_width, b_width=b_width)


def kernel(x_prompt, x_sample, state_rwkv, state_delta, c, c_ctx, w_mod, b_mod, g_pre, g_post, w_in, w0, w_up, a0,
           a_up, k_k, k_a, r_k, gn_w, gn_b, conv_w, a_log, dt_bias, o_norm_w, w_pa, w_pb, w_o):
    bp, tp, dm = x_prompt.shape
    bs, ts, _ = x_sample.shape
    depth = w_mod.shape[0]
    a_width = k_k.shape[-1]
    n_bh = a_log.shape[-1]
    b_width = o_norm_w.shape[-1] * n_bh
    lora = w_up.shape[2]
    n_dir = 2
    rows = ts // GRID_W
    assert o_norm_w.shape[-1] == B_HEAD_DIM and a_width % LANES == 0 and 4 * n_bh <= LANES

    sizes = (a_width,) * 4 + (lora,) * 4 + (b_width,) * 4 + (n_dir * n_bh,) * 2 + (dm,) * 2
    offs = [0]
    for sz in sizes:
        offs.append(offs[-1] + sz)
    lora_w = 4 * lora
    lora_off = 4 * a_width + 4 * b_width + 2 * dm
    tail_off = lora_off + lora_w
    tail_used = 2 * n_dir * n_bh
    assert lora_off % lora_w == 0 and tail_off % LANES == 0
    dims = dict(a_width=a_width, b_width=b_width, lora_off=lora_off, tail_off=tail_off)

    def pack_w_in(w):
        pad = jnp.zeros((w.shape[0], LANES - tail_used), w.dtype)
        return jnp.concatenate([w[:, offs[0]:offs[4]], w[:, offs[8]:offs[12]], w[:, offs[14]:offs[16]],
                                w[:, offs[4]:offs[8]], w[:, offs[12]:offs[14]], pad], axis=1).astype(BF16)

    def pad_up(w, first):
        z = jnp.zeros((n_dir, lora_w, a_width), w.dtype)
        for d in range(n_dir):
            z = z.at[d, (first + d) * lora:(first + d + 1) * lora].set(w[d])
        return z.astype(BF16)

    def tail_vec(v):
        z = jnp.zeros((1, LANES), F32)
        return z.at[0, n_dir * n_bh:2 * n_dir * n_bh].set(v.reshape(-1))

    cvec = jnp.concatenate([c_ctx[None], c, jnp.zeros((8 - 1 - bs, dm), F32)], axis=0)
    mods = _modulation(cvec, w_mod, b_mod)

    xp = x_prompt.reshape(1, bp * tp, dm)
    xs = x_sample
    new_r = jnp.zeros((bp, depth, n_dir, a_width // A_HEAD_DIM, A_HEAD_DIM, A_HEAD_DIM), F32)
    new_d = jnp.zeros((bp, depth, n_dir, n_bh, B_HEAD_DIM, B_HEAD_DIM), F32)
    for l in range(depth):
        lp = dict(
            g_pre=g_pre[l][None], g_post=g_post[l][None], w_in=pack_w_in(w_in[l]),
            wup_pad=pad_up(w_up[l], 0), aup_pad=pad_up(a_up[l], 2),
            w0=w0[l][:, None], a0=a0[l][:, None], k_k=k_k[l][None], k_a=k_a[l][None], r_k=r_k[l][None],
            gn_w=gn_w[l][None], gn_b=gn_b[l][None], conv_w=conv_w[l],
            avec=tail_vec(a_log[l]), dvec=tail_vec(dt_bias[l]),
            onw=jnp.tile(o_norm_w[l], n_bh)[None],
            w_pa=w_pa[l].astype(BF16), w_pb=w_pb[l].astype(BF16), w_o=w_o[l].astype(BF16))
        mod_ctx = mods[l, 0:1][None]
        xp3 = xp.reshape(bp, tp, dm)
        xp, new_r, new_d = _mixer_layer_ctx(xp3, mod_ctx, lp, dims, new_r, new_d, l)
        mod_lat = mods[l, 1:1 + bs][:, None]
        s_r0 = jnp.stack([_pair_block_diag(state_rwkv[:, l, d]) for d in range(n_dir)], axis=1)
        s_d0 = state_delta[:, l]
        if l % 2 == 1:
            xs = _grid_transpose(xs, rows, GRID_W)
        xs = _mixer_layer(xs, mod_lat, s_r0, s_d0, lp, dims)
        if l % 2 == 1:
            xs = _grid_transpose(xs, GRID_W, rows)
    return (xp.reshape(bp, tp, dm), xs, new_r, new_d)


def _mixer_layer_ctx(x, mod, lp, dims, acc_r, acc_d, layer):
    b, t, d = x.shape
    a_width, b_width = dims['a_width'], dims['b_width']
    proj = _in_proj(x.reshape(1, b * t, d), mod, lp['g_pre'], lp['w_in']).reshape(b, t, -1)
    qkv = _gdn_prep(proj, lp['conv_w'], b_width=b_width, qkv_off=4 * a_width)
    ys, os_ = [], []
    for dd, rev in ((0, False), (1, True)):
        y, acc_r = _rwkv_scan(proj, dd, rev, lp['wup_pad'], lp['aup_pad'], lp['w0'], lp['a0'], lp['k_k'],
                              lp['k_a'], lp['r_k'], None, acc_r, layer, a_width=a_width, lora_off=dims['lora_off'])
        o, acc_d = _gdn_scan(qkv, proj, dd, rev, lp['avec'], lp['dvec'], None, acc_d, layer,
                             b_width=b_width, tail_off=dims['tail_off'])
        ys.append(y)
        os_.append(o)
    flat = lambda z: z.reshape(1, b * t, z.shape[-1])
    x_new = _post(flat(ys[0]), flat(ys[1]), flat(os_[0]), flat(os_[1]), flat(proj), flat(x), mod,
                  lp['gn_w'], lp['gn_b'], lp['onw'], lp['g_post'], lp['w_pa'], lp['w_pb'], lp['w_o'],
                  a_width=a_width, b_width=b_width)
    return x_new.reshape(b, t, d), acc_r, acc_d
```

```python
import functools

import jax
import jax.numpy as jnp
from jax import lax
from jax.experimental import pallas as pl
from jax.experimental.pallas import tpu as pltpu

F32 = jnp.float32
BF16 = jnp.bfloat16

GRID_W = 64
A_HEAD_DIM = 64
B_HEAD_DIM = 128
CHUNK = 64
SCAN_TILES = 8
TRI_BASE = 16
DECAY_SCALE = 0.606531
RWKV_GN_EPS = 64e-5
NORM_EPS = 1e-6
LANES = 128
MXU_WIDTH = 256
VMEM_LIMIT = 56 * 1024 * 1024
PREP_BLOCK_BYTES = 2 * 1024 * 1024

NN = (((1,), (0,)), ((), ()))
NT = (((1,), (1,)), ((), ()))
TN = (((0,), (0,)), ((), ()))


def _dot(a, b, dims=NN, hi=False):
    if hi:
        return lax.dot_general(a, b, dims, precision=lax.Precision.HIGHEST, preferred_element_type=F32)
    return lax.dot_general(a.astype(BF16), b.astype(BF16), dims, preferred_element_type=F32)


def _sigmoid(x):
    return jax.nn.sigmoid(x)


def _silu(x):
    return x * jax.nn.sigmoid(x)


def _params(sem):
    return pltpu.CompilerParams(dimension_semantics=sem, vmem_limit_bytes=VMEM_LIMIT)


def _pick(n, cands):
    for c in cands:
        if n % c == 0:
            return c
    raise ValueError(f"no tile for {n}")


def _mod_kernel(c_ref, w_ref, b_ref, o_ref):
    o_ref[0] = _dot(_silu(c_ref[...]), w_ref[0]) + b_ref[0]


def _modulation(cvec, w_mod, b_mod):
    depth, d, n = w_mod.shape
    tn = _pick(n, (512, 256, 128))
    return pl.pallas_call(
        _mod_kernel,
        out_shape=jax.ShapeDtypeStruct((depth, 8, n), F32),
        grid=(depth, n // tn),
        in_specs=[pl.BlockSpec((8, d), lambda l, j: (0, 0)),
                  pl.BlockSpec((1, d, tn), lambda l, j: (l, 0, j)),
                  pl.BlockSpec((1, 1, tn), lambda l, j: (l, 0, j))],
        out_specs=pl.BlockSpec((1, 8, tn), lambda l, j: (l, 0, j)),
        compiler_params=_params(("parallel", "parallel")),
        name="modulation",
    )(cvec, w_mod, b_mod.reshape(depth, 1, n))


def _inproj_kernel(x_ref, mod_ref, g_ref, w_ref, o_ref, h_scr, *, d):
    @pl.when(pl.program_id(2) == 0)
    def _():
        x = x_ref[0]
        y = x * lax.rsqrt(jnp.mean(x * x, axis=-1, keepdims=True) + NORM_EPS) * g_ref[...]
        shift = mod_ref[0, :, 0:d]
        scale = mod_ref[0, :, d:2 * d]
        h_scr[...] = (y * (1.0 + scale) + shift).astype(BF16)

    o_ref[0] = jnp.dot(h_scr[...], w_ref[...], preferred_element_type=F32)


def _in_proj(x, mod, g_pre, w_packed):
    b, t, d = x.shape
    n = w_packed.shape[1]
    tm = _pick(t, (1024, 512, 256, 128))
    tn = _pick(n, (1280, 1024, 512, 256, 128))
    return pl.pallas_call(
        functools.partial(_inproj_kernel, d=d),
        out_shape=jax.ShapeDtypeStruct((b, t, n), F32),
        grid=(b, t // tm, n // tn),
        in_specs=[pl.BlockSpec((1, tm, d), lambda bi, i, j: (bi, i, 0)),
                  pl.BlockSpec((1, 1, 3 * d), lambda bi, i, j: (bi, 0, 0)),
                  pl.BlockSpec((1, d), lambda bi, i, j: (0, 0)),
                  pl.BlockSpec((d, tn), lambda bi, i, j: (0, j))],
        out_specs=pl.BlockSpec((1, tm, tn), lambda bi, i, j: (bi, i, j)),
        scratch_shapes=[pltpu.VMEM((tm, d), BF16)],
        compiler_params=_params(("parallel", "parallel", "arbitrary")),
        name="in_proj",
    )(x, mod, g_pre, w_packed)


def _tri_masks(n, period, reverse):
    ti = lax.broadcasted_iota(jnp.int32, (n, n), 0) & (period - 1)
    sj = lax.broadcasted_iota(jnp.int32, (n, n), 1) & (period - 1)
    if reverse:
        return sj > ti, sj >= ti
    return sj < ti, sj <= ti


def _unit_tri_inverse(negs, eye):
    n = eye.shape[0]
    row = lax.broadcasted_iota(jnp.int32, (n, n), 0)
    col = lax.broadcasted_iota(jnp.int32, (n, n), 1)

    def same_block(size):
        return (row // size) == (col // size)

    base_mask = same_block(TRI_BASE)
    ps = [jnp.where(base_mask, na, 0.0) for na in negs]
    ts = [eye + p for p in ps]
    steps = TRI_BASE.bit_length() - 2
    ps = [_dot(p, p) for p in ps]
    for i in range(steps):
        if i + 1 < steps:
            tps = [_dot(jnp.concatenate([t, p], axis=0), p) for t, p in zip(ts, ps)]
            ts = [t + tp[0:n] for t, tp in zip(ts, tps)]
            ps = [tp[n:2 * n] for tp in tps]
        else:
            ts = [t + _dot(t, p) for t, p in zip(ts, ps)]
    size = TRI_BASE
    while size < CHUNK:
        cmask = jnp.logical_and(same_block(2 * size), jnp.logical_not(same_block(size)))
        ms = [_dot(t, jnp.where(cmask, na, 0.0)) for t, na in zip(ts, negs)]
        ts = [t + _dot(m, t) for t, m in zip(ts, ms)]
        size *= 2
    return ts


def _masked_cumsum(mask01, x):
    n = x.shape[1]
    hi = x.astype(BF16)
    lo = (x - hi.astype(F32)).astype(BF16)
    both = lax.dot_general(mask01.astype(BF16), jnp.concatenate([hi, lo], axis=1), NN, preferred_element_type=F32)
    return both[:, 0:n] + both[:, n:2 * n]


def _rwkv_kernel(*refs, reverse, n_chunks, zero_init, n_tiles, lora_split):
    (r_ref, k_ref, v_ref, lora_ref, wup_ref, aup_ref, w0_ref, a0_ref, kkw_ref, kaw_ref, rkw_ref) = refs[:11]
    if zero_init:
        _, y_ref, sfin_ref, s_scr = refs[11:]
        s0_ref = None
    else:
        s0_ref, y_ref, s_scr = refs[11:]
    i = pl.program_id(2)
    c = CHUNK
    n2 = 2 * c
    width = n_tiles * LANES

    @pl.when(i == 0)
    def _():
        if zero_init:
            s_scr[...] = jnp.zeros_like(s_scr)
        else:
            s_scr[...] = s0_ref[0]

    strict, incl = _tri_masks(n2, c, reverse)
    row = lax.broadcasted_iota(jnp.int32, (n2, n2), 0)
    col = lax.broadcasted_iota(jnp.int32, (n2, n2), 1)
    eye = (row == col).astype(F32)
    sw = min(width, MXU_WIDTH)
    wrow = lax.broadcasted_iota(jnp.int32, (sw, sw), 0)
    wcol = lax.broadcasted_iota(jnp.int32, (sw, sw), 1)
    seg = ((wrow // A_HEAD_DIM) == (wcol // A_HEAD_DIM)).astype(BF16)
    head0 = lax.broadcasted_iota(jnp.int32, (c, LANES), 1) < A_HEAD_DIM

    def stack(z):
        return jnp.concatenate([jnp.where(head0, z, 0.0), jnp.where(head0, 0.0, z)], axis=0).astype(BF16)

    r_all = r_ref[0]
    k_all = k_ref[0]
    v_all = v_ref[0]
    lo = lora_ref[0]
    tb = r_all.shape[0]
    w_hi = -(-lora_split // LANES) * LANES
    a_lo = (lora_split // LANES) * LANES
    logw = -DECAY_SCALE * _sigmoid(w0_ref[0] + _dot(jnp.tanh(lo[:, 0:w_hi]), wup_ref[0, 0:w_hi, :]))
    a = _sigmoid(a0_ref[0] + _dot(lo[:, a_lo:], aup_ref[0, a_lo:, :]))
    kx = k_all * kkw_ref[...]
    kd = k_all * (1.0 + (a - 1.0) * kaw_ref[...])
    stat_in = jnp.concatenate([kx * kx, r_all * kd * rkw_ref[...]], axis=0).astype(BF16)
    sums = jnp.concatenate([_dot(stat_in[:, q * sw:(q + 1) * sw], seg) for q in range(width // sw)], axis=1)
    kk = kx * lax.rsqrt(sums[0:tb] + NORM_EPS)
    bonus = sums[tb:2 * tb] * v_all
    bb = kk * a
    trow = lax.broadcasted_iota(jnp.int32, (tb, tb), 0)
    tcol = lax.broadcasted_iota(jnp.int32, (tb, tb), 1)
    same_chunk = (trow // c) == (tcol // c)
    cmask = jnp.logical_and(same_chunk, (tcol >= trow) if reverse else (tcol <= trow))
    cs = _masked_cumsum(cmask, logw)
    einv = jnp.exp(-cs)
    rt = r_all * jnp.exp(cs)
    kkt = kk * jnp.exp(cs - logw)
    bt = bb * einv
    kt = kd * einv

    chunks = list(range(n_chunks - 1, -1, -1) if reverse else range(n_chunks))
    probs = [(slice(ci * c, (ci + 1) * c), g, slice(g * LANES, (g + 1) * LANES))
             for ci in chunks for g in range(n_tiles)]
    rt_s = [stack(rt[sl, ln]) for sl, _, ln in probs]
    kkt_s = [stack(kkt[sl, ln]) for sl, _, ln in probs]
    bt_s = [stack(bt[sl, ln]) for sl, _, ln in probs]
    kt_s = [stack(kt[sl, ln]) for sl, _, ln in probs]
    v_s = [stack(v_all[sl, ln]) for sl, _, ln in probs]
    ptot = [jnp.exp(jnp.sum(logw[sl, ln], axis=0, keepdims=True)) for sl, _, ln in probs]

    big = [_dot(jnp.concatenate([x, y], axis=0), jnp.concatenate([z, w], axis=0), NT)
           for x, y, z, w in zip(kkt_s, rt_s, bt_s, kt_s)]
    tinv = _unit_tri_inverse([jnp.where(strict, -g[0:n2, 0:n2], 0.0) for g in big], eye)
    av = [_dot(jnp.concatenate([jnp.where(strict, g[0:n2, n2:2 * n2], 0.0),
                                jnp.where(incl, g[n2:2 * n2, n2:2 * n2], 0.0)], axis=0), vs)
          for g, vs in zip(big, v_s)]
    rbm = [jnp.where(incl, g[n2:2 * n2, 0:n2], 0.0).astype(BF16) for g in big]
    wu = [_dot(t, jnp.concatenate([x, y[0:n2].astype(BF16)], axis=1)) for t, x, y in zip(tinv, kkt_s, av)]
    wr = [jnp.concatenate([x[:, 0:n2].astype(BF16), y], axis=0) for x, y in zip(wu, rt_s)]
    u0 = [x[:, n2:2 * n2] for x in wu]
    o0 = [y[n2:2 * n2] for y in av]
    gd = [_dot(jnp.concatenate([x[:, 0:n2], -x[:, n2:2 * n2]], axis=1), z, TN) for x, z in zip(wu, bt_s)]
    dmat = [_dot(vs, ks, TN) + g[n2:2 * n2] for vs, ks, g in zip(v_s, kt_s, gd)]
    gmat = [g[0:n2].astype(BF16) for g in gd]

    s = [s_scr[g] for g in range(n_tiles)]
    for j0 in range(0, len(probs), n_tiles):
        group = list(range(j0, j0 + n_tiles))
        s_in = [s[probs[j][1]] for j in group]
        sg = [_dot(si, gmat[j]) for si, j in zip(s_in, group)]
        xr = [_dot(wr[j], si, NT) for si, j in zip(s_in, group)]
        for si, x, j in zip(s_in, sg, group):
            s[probs[j][1]] = (si - x + dmat[j]) * ptot[j]
        u = [-(x[0:n2] + u0[j]) for x, j in zip(xr, group)]
        ru = [_dot(rbm[j], uj) for uj, j in zip(u, group)]
        for x, r_, j in zip(xr, ru, group):
            sl, _, ln = probs[j]
            o_st = x[n2:2 * n2] + r_ + o0[j]
            y_ref[0, sl, ln] = o_st[0:c] + o_st[c:n2] + bonus[sl, ln]
    for g in range(n_tiles):
        s_scr[g] = s[g]

    if zero_init:
        @pl.when(i == pl.num_programs(2) - 1)
        def _():
            hd = A_HEAD_DIM
            for g in range(n_tiles):
                sfin_ref[0, 0, 0, 2 * g] = s[g][0:hd, 0:hd]
                sfin_ref[0, 0, 0, 2 * g + 1] = s[g][hd:2 * hd, hd:2 * hd]


def _rwkv_scan(proj, d, reverse, wup_pad, aup_pad, w0, a0, kkw, kaw, rkw, s0, state_acc, layer, *, a_width, lora_off):
    b, t, _ = proj.shape
    n_pairs = a_width // LANES
    g = _pick(n_pairs, (SCAN_TILES, 1))
    gw = g * LANES
    tb = _pick(t, (256, 128, 64))
    nblk = t // tb
    lw = wup_pad.shape[1]
    ab = a_width // gw

    def tix(i):
        return (nblk - 1 - i) if reverse else i

    def seg_spec(k):
        return pl.BlockSpec((1, tb, gw), lambda bi, p, i: (bi, tix(i), k * ab + p))

    vec_spec = pl.BlockSpec((1, gw), lambda bi, p, i: (0, p))
    dvec_spec = pl.BlockSpec((1, 1, gw), lambda bi, p, i: (d, 0, p))
    up_spec = pl.BlockSpec((1, lw, gw), lambda bi, p, i: (d, 0, p))
    st_spec = pl.BlockSpec((1, g, LANES, LANES), lambda bi, p, i: (bi, p, 0, 0))
    in_specs = [seg_spec(0), seg_spec(1), seg_spec(2),
                pl.BlockSpec((1, tb, lw), lambda bi, p, i: (bi, tix(i), lora_off // lw)),
                up_spec, up_spec, dvec_spec, dvec_spec, vec_spec, vec_spec, vec_spec]
    args = [proj, proj, proj, proj, wup_pad, aup_pad, w0, a0, kkw, kaw, rkw]
    y_shape = jax.ShapeDtypeStruct((b, t, a_width), F32)
    y_spec = pl.BlockSpec((1, tb, gw), lambda bi, p, i: (bi, tix(i), p))
    if s0 is not None:
        in_specs.append(st_spec)
        args.append(s0)
        out_shape, out_specs, aliases = y_shape, y_spec, {}
    else:
        in_specs.append(pl.BlockSpec(memory_space=pl.ANY))
        args.append(state_acc)
        hd = A_HEAD_DIM
        acc_spec = pl.BlockSpec((1, 1, 1, 2 * g, hd, hd), lambda bi, p, i: (bi, layer, d, p, 0, 0))
        out_shape = (y_shape, jax.ShapeDtypeStruct(state_acc.shape, state_acc.dtype))
        out_specs = (y_spec, acc_spec)
        aliases = {len(args) - 1: 1}
    out = pl.pallas_call(
        functools.partial(_rwkv_kernel, reverse=reverse, n_chunks=tb // CHUNK, zero_init=s0 is None, n_tiles=g,
                          lora_split=lw // 2),
        out_shape=out_shape,
        grid=(b, n_pairs // g, nblk),
        in_specs=in_specs,
        out_specs=out_specs,
        input_output_aliases=aliases,
        scratch_shapes=[pltpu.VMEM((g, LANES, LANES), F32)],
        compiler_params=_params(("parallel", "parallel", "arbitrary")),
        name="rwkv_bwd" if reverse else "rwkv_fwd",
    )(*args)
    return (out, None) if s0 is not None else out


def _gdn_prep_kernel(x_ref, cw_ref, o_ref, *, blocks_per_part):
    part = pl.program_id(1) // blocks_per_part
    t = x_ref.shape[1]
    rowi = lax.broadcasted_iota(jnp.int32, (t, LANES), 0)
    first = rowi == 0
    last = rowi == t - 1
    q_scale = jnp.where(part == 0, B_HEAD_DIM ** -0.5, 1.0)
    for k in range(x_ref.shape[2] // LANES):
        sl = slice(k * LANES, (k + 1) * LANES)
        x = x_ref[0, :, sl]
        xm = jnp.where(first, 0.0, pltpu.roll(x, 1, 0))
        xp = jnp.where(last, 0.0, pltpu.roll(x, t - 1, 0))
        y = _silu(xm * cw_ref[0:1, sl] + x * cw_ref[1:2, sl] + xp * cw_ref[2:3, sl])
        nrm = y * (lax.rsqrt(jnp.sum(y * y, axis=-1, keepdims=True) + NORM_EPS) * q_scale)
        o_ref[0, :, sl] = jnp.where(part < 2, nrm, y)


def _gdn_prep(proj, conv_w, *, b_width, qkv_off):
    b, t, _ = proj.shape
    cw = _pick(b_width, tuple(w for w in (1024, 512, 256, 128) if w * t * 4 <= PREP_BLOCK_BYTES) + (LANES,))
    nblk = b_width // cw
    return pl.pallas_call(
        functools.partial(_gdn_prep_kernel, blocks_per_part=nblk),
        out_shape=jax.ShapeDtypeStruct((b, t, 3 * b_width), F32),
        grid=(b, 3 * nblk),
        in_specs=[pl.BlockSpec((1, t, cw), lambda bi, j: (bi, 0, qkv_off // cw + j)),
                  pl.BlockSpec((3, cw), lambda bi, j: (0, j))],
        out_specs=pl.BlockSpec((1, t, cw), lambda bi, j: (bi, 0, j)),
        compiler_params=_params(("parallel", "parallel")),
        name="gdn_prep",
    )(proj, conv_w)


def _softplus(x):
    return jnp.maximum(x, 0.0) + jnp.log1p(jnp.exp(-jnp.abs(x)))


def _gdn_kernel(*refs, reverse, n_chunks, zero_init, d, n_heads, n_tiles):
    q_ref, k_ref, v_ref, tail_ref, avec_ref, dvec_ref = refs[:6]
    if zero_init:
        _, o_ref, sfin_ref, s_scr = refs[6:]
        s0_ref = None
    else:
        s0_ref, o_ref, s_scr = refs[6:]
    h0 = pl.program_id(1) * n_tiles
    i = pl.program_id(2)
    c = CHUNK

    @pl.when(i == 0)
    def _():
        if zero_init:
            s_scr[...] = jnp.zeros_like(s_scr)
        else:
            s_scr[...] = s0_ref[0]

    strict, incl = _tri_masks(c, c, reverse)
    row = lax.broadcasted_iota(jnp.int32, (c, c), 0)
    col = lax.broadcasted_iota(jnp.int32, (c, c), 1)
    eye = (row == col).astype(F32)

    tail = tail_ref[0]
    tb = tail.shape[0]
    lane = lax.broadcasted_iota(jnp.int32, (tb, LANES), 1)
    beta_all = _sigmoid(tail)
    g_all = -jnp.exp(avec_ref[...]) * _softplus(tail + dvec_ref[...])
    pos = lax.broadcasted_iota(jnp.int32, (tb, LANES), 0) & (c - 1)
    keep = jnp.logical_or(lane >= c, (pos < lane) if reverse else (pos > lane))
    trow = lax.broadcasted_iota(jnp.int32, (tb, tb), 0)
    tcol = lax.broadcasted_iota(jnp.int32, (tb, tb), 1)
    same_chunk = (trow // c) == (tcol // c)
    cmask = jnp.logical_and(same_chunk, (tcol >= trow) if reverse else (tcol <= trow))
    beta, gcol, gpack = [], [], []
    for t_ in range(n_tiles):
        h = h0 + t_
        beta.append(jnp.sum(jnp.where(lane == d * n_heads + h, beta_all, 0.0), axis=-1, keepdims=True))
        gcol.append(jnp.sum(jnp.where(lane == (2 + d) * n_heads + h, g_all, 0.0), axis=-1, keepdims=True))
        gpack.append(jnp.where(keep, jnp.broadcast_to(gcol[t_], (tb, LANES)), 0.0))
    cum = _masked_cumsum(cmask, jnp.concatenate(gpack, axis=1))
    q_all = q_ref[0]
    k_all = k_ref[0]
    v_all = v_ref[0]
    diff, gcum, kb, vb_kbe, qg = [], [], [], [], []
    for t_ in range(n_tiles):
        ln = slice(t_ * LANES, (t_ + 1) * LANES)
        diff.append(cum[:, t_ * LANES:t_ * LANES + c])
        gcum.append(jnp.broadcast_to(cum[:, t_ * LANES + c:t_ * LANES + c + 1], (tb, LANES)))
        egc = jnp.exp(gcum[t_])
        kb.append(k_all[:, ln] * beta[t_])
        vb_kbe.append(jnp.concatenate([v_all[:, ln] * beta[t_], kb[t_] * egc], axis=1))
        qg.append(q_all[:, ln] * egc)

    chunks = list(range(n_chunks - 1, -1, -1) if reverse else range(n_chunks))
    probs = [(slice(ci * c, (ci + 1) * c), t_, slice(t_ * LANES, (t_ + 1) * LANES))
             for ci in chunks for t_ in range(n_tiles)]
    glast = [jnp.sum(jnp.broadcast_to(gcol[t_][sl], (c, LANES)), axis=0, keepdims=True) for sl, t_, _ in probs]
    kg = [(k_all[sl, ln] * jnp.exp(gl - gcum[t_][sl])).astype(BF16) for (sl, t_, ln), gl in zip(probs, glast)]
    eg = [jnp.exp(gl) for gl in glast]
    decay = [jnp.where(incl, jnp.exp(diff[t_][sl]), 0.0) for sl, t_, _ in probs]

    qkk = [_dot(jnp.concatenate([kb[t_][sl], q_all[sl, ln]], axis=0), k_all[sl, ln], NT) for sl, t_, ln in probs]
    tinv = _unit_tri_inverse([jnp.where(strict, -(x[0:c] * dc), 0.0) for x, dc in zip(qkk, decay)], eye)
    qk = [(x[c:2 * c] * dc).astype(BF16) for x, dc in zip(qkk, decay)]
    uw = [_dot(t, vb_kbe[t_][sl]) for t, (sl, t_, _) in zip(tinv, probs)]
    wq = [jnp.concatenate([x[:, LANES:2 * LANES], qg[t_][sl]], axis=0).astype(BF16)
          for x, (sl, t_, _) in zip(uw, probs)]
    dg = [_dot(kgj, x, TN) for kgj, x in zip(kg, uw)]

    s = [s_scr[t_] for t_ in range(n_tiles)]
    for j0 in range(0, len(probs), n_tiles):
        group = list(range(j0, j0 + n_tiles))
        s_in = [s[probs[j][1]] for j in group]
        gs = [_dot(dg[j][:, LANES:2 * LANES], si) for si, j in zip(s_in, group)]
        ws = [_dot(wq[j], si) for si, j in zip(s_in, group)]
        for si, x, j in zip(s_in, gs, group):
            s[probs[j][1]] = si * eg[j] - x + dg[j][:, 0:LANES]
        v_new = [uw[j][:, 0:LANES] - x[0:c] for x, j in zip(ws, group)]
        qv = [_dot(qk[j], vn) for vn, j in zip(v_new, group)]
        for x, q_, j in zip(ws, qv, group):
            sl, _, ln = probs[j]
            o_ref[0, sl, ln] = x[c:2 * c] + q_
    for t_ in range(n_tiles):
        s_scr[t_] = s[t_]

    if zero_init:
        @pl.when(i == pl.num_programs(2) - 1)
        def _():
            for t_ in range(n_tiles):
                sfin_ref[0, 0, 0, t_] = s[t_]


def _gdn_scan(qkv, proj, d, reverse, avec, dvec, s0, state_acc, layer, *, b_width, tail_off):
    b, t, _ = qkv.shape
    nh = b_width // LANES
    g = _pick(nh, (SCAN_TILES, 1))
    gw = g * LANES
    tb = _pick(t, (256, 128, 64))
    nblk = t // tb
    hb = nh // g

    def tix(i):
        return (nblk - 1 - i) if reverse else i

    def seg_spec(k):
        return pl.BlockSpec((1, tb, gw), lambda bi, h, i: (bi, tix(i), k * hb + h))

    vec_spec = pl.BlockSpec((1, LANES), lambda bi, h, i: (0, 0))
    st_spec = pl.BlockSpec((1, g, LANES, LANES), lambda bi, h, i: (bi, h, 0, 0))
    in_specs = [seg_spec(0), seg_spec(1), seg_spec(2),
                pl.BlockSpec((1, tb, LANES), lambda bi, h, i: (bi, tix(i), tail_off // LANES)),
                vec_spec, vec_spec]
    args = [qkv, qkv, qkv, proj, avec, dvec]
    o_shape = jax.ShapeDtypeStruct((b, t, b_width), F32)
    o_spec = pl.BlockSpec((1, tb, gw), lambda bi, h, i: (bi, tix(i), h))
    if s0 is not None:
        in_specs.append(st_spec)
        args.append(s0)
        out_shape, out_specs, aliases = o_shape, o_spec, {}
    else:
        in_specs.append(pl.BlockSpec(memory_space=pl.ANY))
        args.append(state_acc)
        acc_spec = pl.BlockSpec((1, 1, 1, g, LANES, LANES), lambda bi, h, i: (bi, layer, d, h, 0, 0))
        out_shape = (o_shape, jax.ShapeDtypeStruct(state_acc.shape, state_acc.dtype))
        out_specs = (o_spec, acc_spec)
        aliases = {len(args) - 1: 1}
    out = pl.pallas_call(
        functools.partial(_gdn_kernel, reverse=reverse, n_chunks=tb // CHUNK, zero_init=s0 is None, d=d, n_heads=nh,
                          n_tiles=g),
        out_shape=out_shape,
        grid=(b, hb, nblk),
        in_specs=in_specs,
        out_specs=out_specs,
        input_output_aliases=aliases,
        scratch_shapes=[pltpu.VMEM((g, LANES, LANES), F32)],
        compiler_params=_params(("parallel", "parallel", "arbitrary")),
        name="gdn_bwd" if reverse else "gdn_fwd",
    )(*args)
    return (out, None) if s0 is not None else out


def _post_kernel(yf_ref, yb_ref, za_ref, of_ref, ob_ref, zb_ref, ga_ref, gb_ref, x_ref, mod_ref,
                 gnw_ref, gnb_ref, onw_ref, gpost_ref, wpa_ref, wpb_ref, wo_ref, o_ref, ya_scr, yb_scr, *, d):
    a_width = yf_ref.shape[-1]
    b_width = of_ref.shape[-1]
    row = lax.broadcasted_iota(jnp.int32, (LANES, LANES), 0)
    col = lax.broadcasted_iota(jnp.int32, (LANES, LANES), 1)
    seg_mean = (((row // A_HEAD_DIM) == (col // A_HEAD_DIM)).astype(F32) * (1.0 / A_HEAD_DIM)).astype(BF16)
    tm = yf_ref.shape[1]
    for p in range(a_width // LANES):
        sl = slice(p * LANES, (p + 1) * LANES)
        y = yf_ref[0, :, sl] + yb_ref[0, :, sl]
        y_hi = y.astype(BF16)
        y_lo = (y - y_hi.astype(F32)).astype(BF16)
        mean2 = _dot(jnp.concatenate([y_hi, y_lo], axis=0), seg_mean)
        cen = y - (mean2[0:tm] + mean2[tm:2 * tm])
        var = _dot(cen * cen, seg_mean)
        yn = cen * lax.rsqrt(var + RWKV_GN_EPS) * gnw_ref[:, sl] + gnb_ref[:, sl]
        ya_scr[:, sl] = (yn * _silu(za_ref[0, :, sl])).astype(BF16)
    for hh in range(b_width // LANES):
        sl = slice(hh * LANES, (hh + 1) * LANES)
        o = of_ref[0, :, sl] + ob_ref[0, :, sl]
        on = o * lax.rsqrt(jnp.mean(o * o, axis=-1, keepdims=True) + NORM_EPS) * onw_ref[:, sl]
        yb_scr[:, sl] = (on * _silu(zb_ref[0, :, sl])).astype(BF16)
    branch_a = jnp.dot(ya_scr[...], wpa_ref[...], preferred_element_type=F32)
    branch_b = jnp.dot(yb_scr[...], wpb_ref[...], preferred_element_type=F32)
    merged = _sigmoid(ga_ref[0]) * branch_a + _sigmoid(gb_ref[0]) * branch_b
    out = jnp.dot(merged.astype(BF16), wo_ref[...], preferred_element_type=F32)
    on = out * lax.rsqrt(jnp.mean(out * out, axis=-1, keepdims=True) + NORM_EPS) * gpost_ref[...]
    o_ref[0] = x_ref[0] + mod_ref[0, :, 2 * d:3 * d] * on


def _post(yf, yb, of, ob, proj, x, mod, gnw, gnb, onw, gpost, wpa, wpb, wo, *, a_width, b_width):
    b, t, d = x.shape
    tm = _pick(t, (256, 128, 64))
    za_blk = 3
    zb_blk = (4 * a_width + 3 * b_width) // b_width
    ga_blk = (4 * a_width + 4 * b_width) // d
    assert (4 * a_width + 3 * b_width) % b_width == 0 and (4 * a_width + 4 * b_width) % d == 0

    def row_spec(width, blk=0):
        return pl.BlockSpec((1, tm, width), lambda bi, i: (bi, i, blk))

    def full(shape):
        return pl.BlockSpec(shape, lambda bi, i: (0,) * len(shape), pipeline_mode=pl.Buffered(1))

    return pl.pallas_call(
        functools.partial(_post_kernel, d=d),
        out_shape=jax.ShapeDtypeStruct((b, t, d), F32),
        grid=(b, t // tm),
        in_specs=[row_spec(a_width), row_spec(a_width), row_spec(a_width, za_blk),
                  row_spec(b_width), row_spec(b_width), row_spec(b_width, zb_blk),
                  row_spec(d, ga_blk), row_spec(d, ga_blk + 1), row_spec(d),
                  pl.BlockSpec((1, 1, 3 * d), lambda bi, i: (bi, 0, 0)),
                  full((1, a_width)), full((1, a_width)), full((1, b_width)), full((1, d)),
                  full((a_width, d)), full((b_width, d)), full((d, d))],
        out_specs=row_spec(d),
        scratch_shapes=[pltpu.VMEM((tm, a_width), BF16), pltpu.VMEM((tm, b_width), BF16)],
        compiler_params=_params(("parallel", "parallel")),
        name="post",
    )(yf, yb, proj, of, ob, proj, proj, proj, x, mod, gnw, gnb, onw, gpost, wpa, wpb, wo)


def _grid_transpose(x, rows, cols):
    b, t, d = x.shape
    return x.reshape(b, rows, cols, d).swapaxes(1, 2).reshape(b, t, d)


def _pair_block_diag(s):
    b, h, n, _ = s.shape
    sp = s.reshape(b, h // 2, 2, n, n)
    eye2 = jnp.eye(2, dtype=s.dtype)
    return jnp.einsum('bpivk,ij->bpivjk', sp, eye2).reshape(b, h // 2, 2 * n, 2 * n)


def _mixer_layer(x, mod, s_rwkv0, s_delta0, lp, dims):
    a_width, b_width = dims['a_width'], dims['b_width']
    proj = _in_proj(x, mod, lp['g_pre'], lp['w_in'])
    ys, os_ = [], []
    qkv = _gdn_prep(proj, lp['conv_w'], b_width=b_width, qkv_off=4 * a_width)
    for d, rev in ((0, False), (1, True)):
        y, _ = _rwkv_scan(proj, d, rev, lp['wup_pad'], lp['aup_pad'], lp['w0'], lp['a0'], lp['k_k'], lp['k_a'],
                          lp['r_k'], s_rwkv0[:, d], None, None, a_width=a_width, lora_off=dims['lora_off'])
        o, _ = _gdn_scan(qkv, proj, d, rev, lp['avec'], lp['dvec'], s_delta0[:, d], None, None,
                         b_width=b_width, tail_off=dims['tail_off'])
        ys.append(y)
        os_.append(o)
    return _post(ys[0], ys[1], os_[0], os_[1], proj, x, mod, lp['gn_w'], lp['gn_b'], lp['onw'], lp['g_post'],
                 lp['w_pa'], lp['w_pb'], lp['w_o'], a_width=a_width, b_width=b_width)


def kernel(x_prompt, x_sample, state_rwkv, state_delta, c, c_ctx, w_mod, b_mod, g_pre, g_post, w_in, w0, w_up, a0,
           a_up, k_k, k_a, r_k, gn_w, gn_b, conv_w, a_log, dt_bias, o_norm_w, w_pa, w_pb, w_o):
    bp, tp, dm = x_prompt.shape
    bs, ts, _ = x_sample.shape
    depth = w_mod.shape[0]
    a_width = k_k.shape[-1]
    n_bh = a_log.shape[-1]
    b_width = o_norm_w.shape[-1] * n_bh
    lora = w_up.shape[2]
    n_dir = 2
    rows = ts // GRID_W
    assert o_norm_w.shape[-1] == B_HEAD_DIM and a_width % LANES == 0 and 4 * n_bh <= LANES

    sizes = (a_width,) * 4 + (lora,) * 4 + (b_width,) * 4 + (n_dir * n_bh,) * 2 + (dm,) * 2
    offs = [0]
    for sz in sizes:
        offs.append(offs[-1] + sz)
    lora_w = 4 * lora
    lora_off = 4 * a_width + 4 * b_width + 2 * dm
    tail_off = lora_off + lora_w
    tail_used = 2 * n_dir * n_bh
    assert lora_off % lora_w == 0 and tail_off % LANES == 0
    dims = dict(a_width=a_width, b_width=b_width, lora_off=lora_off, tail_off=tail_off)

    def pack_w_in(w):
        pad = jnp.zeros((w.shape[0], LANES - tail_used), w.dtype)
        return jnp.concatenate([w[:, offs[0]:offs[4]], w[:, offs[8]:offs[12]], w[:, offs[14]:offs[16]],
                                w[:, offs[4]:offs[8]], w[:, offs[12]:offs[14]], pad], axis=1).astype(BF16)

    def pad_up(w, first):
        z = jnp.zeros((n_dir, lora_w, a_width), w.dtype)
        for d in range(n_dir):
            z = z.at[d, (first + d) * lora:(first + d + 1) * lora].set(w[d])
        return z.astype(BF16)

    def tail_vec(v):
        z = jnp.zeros((1, LANES), F32)
        return z.at[0, n_dir * n_bh:2 * n_dir * n_bh].set(v.reshape(-1))

    cvec = jnp.concatenate([c_ctx[None], c, jnp.zeros((8 - 1 - bs, dm), F32)], axis=0)
    mods = _modulation(cvec, w_mod, b_mod)

    xp = x_prompt.reshape(1, bp * tp, dm)
    xs = x_sample
    new_r = jnp.zeros((bp, depth, n_dir, a_width // A_HEAD_DIM, A_HEAD_DIM, A_HEAD_DIM), F32)
    new_d = jnp.zeros((bp, depth, n_dir, n_bh, B_HEAD_DIM, B_HEAD_DIM), F32)
    for l in range(depth):
        lp = dict(
            g_pre=g_pre[l][None], g_post=g_post[l][None], w_in=pack_w_in(w_in[l]),
            wup_pad=pad_up(w_up[l], 0), aup_pad=pad_up(a_up[l], 2),
            w0=w0[l][:, None], a0=a0[l][:, None], k_k=k_k[l][None], k_a=k_a[l][None], r_k=r_k[l][None],
            gn_w=gn_w[l][None], gn_b=gn_b[l][None], conv_w=conv_w[l],
            avec=tail_vec(a_log[l]), dvec=tail_vec(dt_bias[l]),
            onw=jnp.tile(o_norm_w[l], n_bh)[None],
            w_pa=w_pa[l].astype(BF16), w_pb=w_pb[l].astype(BF16), w_o=w_o[l].astype(BF16))
        mod_ctx = mods[l, 0:1][None]
        xp3 = xp.reshape(bp, tp, dm)
        xp, new_r, new_d = _mixer_layer_ctx(xp3, mod_ctx, lp, dims, new_r, new_d, l)
        mod_lat = mods[l, 1:1 + bs][:, None]
        s_r0 = jnp.stack([_pair_block_diag(state_rwkv[:, l, d]) for d in range(n_dir)], axis=1)
        s_d0 = state_delta[:, l]
        if l % 2 == 1:
            xs = _grid_transpose(xs, rows, GRID_W)
        xs = _mixer_layer(xs, mod_lat, s_r0, s_d0, lp, dims)
        if l % 2 == 1:
            xs = _grid_transpose(xs, GRID_W, rows)
    return (xp.reshape(bp, tp, dm), xs, new_r, new_d)


def _mixer_layer_ctx(x, mod, lp, dims, acc_r, acc_d, layer):
    b, t, d = x.shape
    a_width, b_width = dims['a_width'], dims['b_width']
    proj = _in_proj(x.reshape(1, b * t, d), mod, lp['g_pre'], lp['w_in']).reshape(b, t, -1)
    qkv = _gdn_prep(proj, lp['conv_w'], b_width=b_width, qkv_off=4 * a_width)
    ys, os_ = [], []
    for dd, rev in ((0, False), (1, True)):
        y, acc_r = _rwkv_scan(proj, dd, rev, lp['wup_pad'], lp['aup_pad'], lp['w0'], lp['a0'], lp['k_k'],
                              lp['k_a'], lp['r_k'], None, acc_r, layer, a_width=a_width, lora_off=dims['lora_off'])
        o, acc_d = _gdn_scan(qkv, proj, dd, rev, lp['avec'], lp['dvec'], None, acc_d, layer,
                             b_width=b_width, tail_off=dims['tail_off'])
        ys.append(y)
        os_.append(o)
    flat = lambda z: z.reshape(1, b * t, z.shape[-1])
    x_new = _post(flat(ys[0]), flat(ys[1]), flat(os_[0]), flat(os_[1]), flat(proj), flat(x), mod,
                  lp['gn_w'], lp['gn_b'], lp['onw'], lp['g_post'], lp['w_pa'], lp['w_pb'], lp['w_o'],
                  a_width=a_width, b_width=b_width)
    return x_new.reshape(b, t, d), acc_r, acc_d
```

```python
import functools

import jax
import jax.numpy as jnp
from jax import lax
from jax.experimental import pallas as pl
from jax.experimental.pallas import tpu as pltpu

F32 = jnp.float32
BF16 = jnp.bfloat16

GRID_W = 64
A_HEAD_DIM = 64
B_HEAD_DIM = 128
CHUNK = 64
SCAN_TILES = 8
TRI_BASE = 16
DECAY_SCALE = 0.606531
RWKV_GN_EPS = 64e-5
NORM_EPS = 1e-6
LANES = 128
MXU_WIDTH = 256
VMEM_LIMIT = 56 * 1024 * 1024
PREP_BLOCK_BYTES = 2 * 1024 * 1024

NN = (((1,), (0,)), ((), ()))
NT = (((1,), (1,)), ((), ()))
TN = (((0,), (0,)), ((), ()))


def _dot(a, b, dims=NN, hi=False):
    if hi:
        return lax.dot_general(a, b, dims, precision=lax.Precision.HIGHEST, preferred_element_type=F32)
    return lax.dot_general(a.astype(BF16), b.astype(BF16), dims, preferred_element_type=F32)


def _sigmoid(x):
    return jax.nn.sigmoid(x)


def _silu(x):
    return x * jax.nn.sigmoid(x)


def _params(sem):
    return pltpu.CompilerParams(dimension_semantics=sem, vmem_limit_bytes=VMEM_LIMIT)


def _pick(n, cands):
    for c in cands:
        if n % c == 0:
            return c
    raise ValueError(f"no tile for {n}")


def _mod_kernel(c_ref, w_ref, b_ref, o_ref):
    o_ref[0] = _dot(_silu(c_ref[...]), w_ref[0]) + b_ref[0]


def _modulation(cvec, w_mod, b_mod):
    depth, d, n = w_mod.shape
    tn = _pick(n, (512, 256, 128))
    return pl.pallas_call(
        _mod_kernel,
        out_shape=jax.ShapeDtypeStruct((depth, 8, n), F32),
        grid=(depth, n // tn),
        in_specs=[pl.BlockSpec((8, d), lambda l, j: (0, 0)),
                  pl.BlockSpec((1, d, tn), lambda l, j: (l, 0, j)),
                  pl.BlockSpec((1, 1, tn), lambda l, j: (l, 0, j))],
        out_specs=pl.BlockSpec((1, 8, tn), lambda l, j: (l, 0, j)),
        compiler_params=_params(("parallel", "parallel")),
        name="modulation",
    )(cvec, w_mod, b_mod.reshape(depth, 1, n))


def _inproj_kernel(x_ref, mod_ref, g_ref, w_ref, o_ref, h_scr, *, d):
    @pl.when(pl.program_id(2) == 0)
    def _():
        x = x_ref[0]
        y = x * lax.rsqrt(jnp.mean(x * x, axis=-1, keepdims=True) + NORM_EPS) * g_ref[...]
        shift = mod_ref[0, :, 0:d]
        scale = mod_ref[0, :, d:2 * d]
        h_scr[...] = (y * (1.0 + scale) + shift).astype(BF16)

    o_ref[0] = jnp.dot(h_scr[...], w_ref[...], preferred_element_type=F32)


def _in_proj(x, mod, g_pre, w_packed):
    b, t, d = x.shape
    n = w_packed.shape[1]
    tm = _pick(t, (1024, 512, 256, 128))
    tn = _pick(n, (1280, 1024, 512, 256, 128))
    return pl.pallas_call(
        functools.partial(_inproj_kernel, d=d),
        out_shape=jax.ShapeDtypeStruct((b, t, n), F32),
        grid=(b, t // tm, n // tn),
        in_specs=[pl.BlockSpec((1, tm, d), lambda bi, i, j: (bi, i, 0)),
                  pl.BlockSpec((1, 1, 3 * d), lambda bi, i, j: (bi, 0, 0)),
                  pl.BlockSpec((1, d), lambda bi, i, j: (0, 0)),
                  pl.BlockSpec((d, tn), lambda bi, i, j: (0, j))],
        out_specs=pl.BlockSpec((1, tm, tn), lambda bi, i, j: (bi, i, j)),
        scratch_shapes=[pltpu.VMEM((tm, d), BF16)],
        compiler_params=_params(("parallel", "parallel", "arbitrary")),
        name="in_proj",
    )(x, mod, g_pre, w_packed)


def _tri_masks(n, period, reverse):
    ti = lax.broadcasted_iota(jnp.int32, (n, n), 0) & (period - 1)
    sj = lax.broadcasted_iota(jnp.int32, (n, n), 1) & (period - 1)
    if reverse:
        return sj > ti, sj >= ti
    return sj < ti, sj <= ti


def _unit_tri_inverse(negs, eye):
    n = eye.shape[0]
    row = lax.broadcasted_iota(jnp.int32, (n, n), 0)
    col = lax.broadcasted_iota(jnp.int32, (n, n), 1)

    def same_block(size):
        return (row // size) == (col // size)

    base_mask = same_block(TRI_BASE)
    ps = [jnp.where(base_mask, na, 0.0) for na in negs]
    ts = [eye + p for p in ps]
    steps = TRI_BASE.bit_length() - 2
    ps = [_dot(p, p) for p in ps]
    for i in range(steps):
        if i + 1 < steps:
            tps = [_dot(jnp.concatenate([t, p], axis=0), p) for t, p in zip(ts, ps)]
            ts = [t + tp[0:n] for t, tp in zip(ts, tps)]
            ps = [tp[n:2 * n] for tp in tps]
        else:
            ts = [t + _dot(t, p) for t, p in zip(ts, ps)]
    size = TRI_BASE
    while size < CHUNK:
        cmask = jnp.logical_and(same_block(2 * size), jnp.logical_not(same_block(size)))
        ms = [_dot(t, jnp.where(cmask, na, 0.0)) for t, na in zip(ts, negs)]
        ts = [t + _dot(m, t) for t, m in zip(ts, ms)]
        size *= 2
    return ts


def _masked_cumsum(mask01, x):
    n = x.shape[1]
    hi = x.astype(BF16)
    lo = (x - hi.astype(F32)).astype(BF16)
    both = lax.dot_general(mask01.astype(BF16), jnp.concatenate([hi, lo], axis=1), NN, preferred_element_type=F32)
    return both[:, 0:n] + both[:, n:2 * n]


def _rwkv_kernel(*refs, reverse, n_chunks, zero_init, n_tiles, lora_split):
    (r_ref, k_ref, v_ref, lora_ref, wup_ref, aup_ref, w0_ref, a0_ref, kkw_ref, kaw_ref, rkw_ref) = refs[:11]
    if zero_init:
        _, y_ref, sfin_ref, s_scr = refs[11:]
        s0_ref = None
    else:
        s0_ref, y_ref, s_scr = refs[11:]
    i = pl.program_id(2)
    c = CHUNK
    n2 = 2 * c
    width = n_tiles * LANES

    @pl.when(i == 0)
    def _():
        if zero_init:
            s_scr[...] = jnp.zeros_like(s_scr)
        else:
            s_scr[...] = s0_ref[0]

    strict, incl = _tri_masks(n2, c, reverse)
    row = lax.broadcasted_iota(jnp.int32, (n2, n2), 0)
    col = lax.broadcasted_iota(jnp.int32, (n2, n2), 1)
    eye = (row == col).astype(F32)
    sw = min(width, MXU_WIDTH)
    wrow = lax.broadcasted_iota(jnp.int32, (sw, sw), 0)
    wcol = lax.broadcasted_iota(jnp.int32, (sw, sw), 1)
    seg = ((wrow // A_HEAD_DIM) == (wcol // A_HEAD_DIM)).astype(BF16)
    head0 = lax.broadcasted_iota(jnp.int32, (c, LANES), 1) < A_HEAD_DIM

    def stack(z):
        return jnp.concatenate([jnp.where(head0, z, 0.0), jnp.where(head0, 0.0, z)], axis=0).astype(BF16)

    r_all = r_ref[0]
    k_all = k_ref[0]
    v_all = v_ref[0]
    lo = lora_ref[0]
    tb = r_all.shape[0]
    w_hi = -(-lora_split // LANES) * LANES
    a_lo = (lora_split // LANES) * LANES
    logw = -DECAY_SCALE * _sigmoid(w0_ref[0] + _dot(jnp.tanh(lo[:, 0:w_hi]), wup_ref[0, 0:w_hi, :]))
    a = _sigmoid(a0_ref[0] + _dot(lo[:, a_lo:], aup_ref[0, a_lo:, :]))
    kx = k_all * kkw_ref[...]
    kd = k_all * (1.0 + (a - 1.0) * kaw_ref[...])
    stat_in = jnp.concatenate([kx * kx, r_all * kd * rkw_ref[...]], axis=0).astype(BF16)
    sums = jnp.concatenate([_dot(stat_in[:, q * sw:(q + 1) * sw], seg) for q in range(width // sw)], axis=1)
    kk = kx * lax.rsqrt(sums[0:tb] + NORM_EPS)
    bonus = sums[tb:2 * tb] * v_all
    bb = kk * a
    trow = lax.broadcasted_iota(jnp.int32, (tb, tb), 0)
    tcol = lax.broadcasted_iota(jnp.int32, (tb, tb), 1)
    same_chunk = (trow // c) == (tcol // c)
    cmask = jnp.logical_and(same_chunk, (tcol >= trow) if reverse else (tcol <= trow))
    cs = _masked_cumsum(cmask, logw)
    einv = jnp.exp(-cs)
    rt = r_all * jnp.exp(cs)
    kkt = kk * jnp.exp(cs - logw)
    bt = bb * einv
    kt = kd * einv

    chunks = list(range(n_chunks - 1, -1, -1) if reverse else range(n_chunks))
    probs = [(slice(ci * c, (ci + 1) * c), g, slice(g * LANES, (g + 1) * LANES))
             for ci in chunks for g in range(n_tiles)]
    rt_s = [stack(rt[sl, ln]) for sl, _, ln in probs]
    kkt_s = [stack(kkt[sl, ln]) for sl, _, ln in probs]
    bt_s = [stack(bt[sl, ln]) for sl, _, ln in probs]
    kt_s = [stack(kt[sl, ln]) for sl, _, ln in probs]
    v_s = [stack(v_all[sl, ln]) for sl, _, ln in probs]
    ptot = [jnp.exp(jnp.sum(logw[sl, ln], axis=0, keepdims=True)) for sl, _, ln in probs]

    big = [_dot(jnp.concatenate([x, y], axis=0), jnp.concatenate([z, w], axis=0), NT)
           for x, y, z, w in zip(kkt_s, rt_s, bt_s, kt_s)]
    tinv = _unit_tri_inverse([jnp.where(strict, -g[0:n2, 0:n2], 0.0) for g in big], eye)
    av = [_dot(jnp.concatenate([jnp.where(strict, g[0:n2, n2:2 * n2], 0.0),
                                jnp.where(incl, g[n2:2 * n2, n2:2 * n2], 0.0)], axis=0), vs)
          for g, vs in zip(big, v_s)]
    rbm = [jnp.where(incl, g[n2:2 * n2, 0:n2], 0.0).astype(BF16) for g in big]
    wu = [_dot(t, jnp.concatenate([x, y[0:n2].astype(BF16)], axis=1)) for t, x, y in zip(tinv, kkt_s, av)]
    wr = [jnp.concatenate([x[:, 0:n2].astype(BF16), y], axis=0) for x, y in zip(wu, rt_s)]
    u0 = [x[:, n2:2 * n2] for x in wu]
    o0 = [y[n2:2 * n2] for y in av]
    gd = [_dot(jnp.concatenate([x[:, 0:n2], -x[:, n2:2 * n2]], axis=1), z, TN) for x, z in zip(wu, bt_s)]
    dmat = [_dot(vs, ks, TN) + g[n2:2 * n2] for vs, ks, g in zip(v_s, kt_s, gd)]
    gmat = [g[0:n2].astype(BF16) for g in gd]

    s = [s_scr[g] for g in range(n_tiles)]
    for j0 in range(0, len(probs), n_tiles):
        group = list(range(j0, j0 + n_tiles))
        s_in = [s[probs[j][1]] for j in group]
        sg = [_dot(si, gmat[j]) for si, j in zip(s_in, group)]
        xr = [_dot(wr[j], si, NT) for si, j in zip(s_in, group)]
        for si, x, j in zip(s_in, sg, group):
            s[probs[j][1]] = (si - x + dmat[j]) * ptot[j]
        u = [-(x[0:n2] + u0[j]) for x, j in zip(xr, group)]
        ru = [_dot(rbm[j], uj) for uj, j in zip(u, group)]
        for x, r_, j in zip(xr, ru, group):
            sl, _, ln = probs[j]
            o_st = x[n2:2 * n2] + r_ + o0[j]
            y_ref[0, sl, ln] = o_st[0:c] + o_st[c:n2] + bonus[sl, ln]
    for g in range(n_tiles):
        s_scr[g] = s[g]

    if zero_init:
        @pl.when(i == pl.num_programs(2) - 1)
        def _():
            hd = A_HEAD_DIM
            for g in range(n_tiles):
                sfin_ref[0, 0, 0, 2 * g] = s[g][0:hd, 0:hd]
                sfin_ref[0, 0, 0, 2 * g + 1] = s[g][hd:2 * hd, hd:2 * hd]


def _rwkv_scan(proj, d, reverse, wup_pad, aup_pad, w0, a0, kkw, kaw, rkw, s0, state_acc, layer, *, a_width, lora_off):
    b, t, _ = proj.shape
    n_pairs = a_width // LANES
    g = _pick(n_pairs, (SCAN_TILES, 1))
    gw = g * LANES
    tb = _pick(t, (256, 128, 64))
    nblk = t // tb
    lw = wup_pad.shape[1]
    ab = a_width // gw

    def tix(i):
        return (nblk - 1 - i) if reverse else i

    def seg_spec(k):
        return pl.BlockSpec((1, tb, gw), lambda bi, p, i: (bi, tix(i), k * ab + p))

    vec_spec = pl.BlockSpec((1, gw), lambda bi, p, i: (0, p))
    dvec_spec = pl.BlockSpec((1, 1, gw), lambda bi, p, i: (d, 0, p))
    up_spec = pl.BlockSpec((1, lw, gw), lambda bi, p, i: (d, 0, p))
    st_spec = pl.BlockSpec((1, g, LANES, LANES), lambda bi, p, i: (bi, p, 0, 0))
    in_specs = [seg_spec(0), seg_spec(1), seg_spec(2),
                pl.BlockSpec((1, tb, lw), lambda bi, p, i: (bi, tix(i), lora_off // lw)),
                up_spec, up_spec, dvec_spec, dvec_spec, vec_spec, vec_spec, vec_spec]
    args = [proj, proj, proj, proj, wup_pad, aup_pad, w0, a0, kkw, kaw, rkw]
    y_shape = jax.ShapeDtypeStruct((b, t, a_width), F32)
    y_spec = pl.BlockSpec((1, tb, gw), lambda bi, p, i: (bi, tix(i), p))
    if s0 is not None:
        in_specs.append(st_spec)
        args.append(s0)
        out_shape, out_specs, aliases = y_shape, y_spec, {}
    else:
        in_specs.append(pl.BlockSpec(memory_space=pl.ANY))
        args.append(state_acc)
        hd = A_HEAD_DIM
        acc_spec = pl.BlockSpec((1, 1, 1, 2 * g, hd, hd), lambda bi, p, i: (bi, layer, d, p, 0, 0))
        out_shape = (y_shape, jax.ShapeDtypeStruct(state_acc.shape, state_acc.dtype))
        out_specs = (y_spec, acc_spec)
        aliases = {len(args) - 1: 1}
    out = pl.pallas_call(
        functools.partial(_rwkv_kernel, reverse=reverse, n_chunks=tb // CHUNK, zero_init=s0 is None, n_tiles=g,
                          lora_split=lw // 2),
        out_shape=out_shape,
        grid=(b, n_pairs // g, nblk),
        in_specs=in_specs,
        out_specs=out_specs,
        input_output_aliases=aliases,
        scratch_shapes=[pltpu.VMEM((g, LANES, LANES), F32)],
        compiler_params=_params(("parallel", "parallel", "arbitrary")),
        name="rwkv_bwd" if reverse else "rwkv_fwd",
    )(*args)
    return (out, None) if s0 is not None else out


def _gdn_prep_kernel(x_ref, cw_ref, o_ref, *, blocks_per_part):
    part = pl.program_id(1) // blocks_per_part
    t = x_ref.shape[1]
    rowi = lax.broadcasted_iota(jnp.int32, (t, LANES), 0)
    first = rowi == 0
    last = rowi == t - 1
    q_scale = jnp.where(part == 0, B_HEAD_DIM ** -0.5, 1.0)
    for k in range(x_ref.shape[2] // LANES):
        sl = slice(k * LANES, (k + 1) * LANES)
        x = x_ref[0, :, sl]
        xm = jnp.where(first, 0.0, pltpu.roll(x, 1, 0))
        xp = jnp.where(last, 0.0, pltpu.roll(x, t - 1, 0))
        y = _silu(xm * cw_ref[0:1, sl] + x * cw_ref[1:2, sl] + xp * cw_ref[2:3, sl])
        nrm = y * (lax.rsqrt(jnp.sum(y * y, axis=-1, keepdims=True) + NORM_EPS) * q_scale)
        o_ref[0, :, sl] = jnp.where(part < 2, nrm, y)


def _gdn_prep(proj, conv_w, *, b_width, qkv_off):
    b, t, _ = proj.shape
    cw = _pick(b_width, tuple(w for w in (1024, 512, 256, 128) if w * t * 4 <= PREP_BLOCK_BYTES) + (LANES,))
    nblk = b_width // cw
    return pl.pallas_call(
        functools.partial(_gdn_prep_kernel, blocks_per_part=nblk),
        out_shape=jax.ShapeDtypeStruct((b, t, 3 * b_width), F32),
        grid=(b, 3 * nblk),
        in_specs=[pl.BlockSpec((1, t, cw), lambda bi, j: (bi, 0, qkv_off // cw + j)),
                  pl.BlockSpec((3, cw), lambda bi, j: (0, j))],
        out_specs=pl.BlockSpec((1, t, cw), lambda bi, j: (bi, 0, j)),
        compiler_params=_params(("parallel", "parallel")),
        name="gdn_prep",
    )(proj, conv_w)


def _softplus(x):
    return jnp.maximum(x, 0.0) + jnp.log1p(jnp.exp(-jnp.abs(x)))


def _gdn_kernel(*refs, reverse, n_chunks, zero_init, d, n_heads, n_tiles):
    q_ref, k_ref, v_ref, tail_ref, avec_ref, dvec_ref = refs[:6]
    if zero_init:
        _, o_ref, sfin_ref, s_scr = refs[6:]
        s0_ref = None
    else:
        s0_ref, o_ref, s_scr = refs[6:]
    h0 = pl.program_id(1) * n_tiles
    i = pl.program_id(2)
    c = CHUNK

    @pl.when(i == 0)
    def _():
        if zero_init:
            s_scr[...] = jnp.zeros_like(s_scr)
        else:
            s_scr[...] = s0_ref[0]

    strict, incl = _tri_masks(c, c, reverse)
    row = lax.broadcasted_iota(jnp.int32, (c, c), 0)
    col = lax.broadcasted_iota(jnp.int32, (c, c), 1)
    eye = (row == col).astype(F32)

    tail = tail_ref[0]
    tb = tail.shape[0]
    lane = lax.broadcasted_iota(jnp.int32, (tb, LANES), 1)
    beta_all = _sigmoid(tail)
    g_all = -jnp.exp(avec_ref[...]) * _softplus(tail + dvec_ref[...])
    pos = lax.broadcasted_iota(jnp.int32, (tb, LANES), 0) & (c - 1)
    keep = jnp.logical_or(lane >= c, (pos < lane) if reverse else (pos > lane))
    trow = lax.broadcasted_iota(jnp.int32, (tb, tb), 0)
    tcol = lax.broadcasted_iota(jnp.int32, (tb, tb), 1)
    same_chunk = (trow // c) == (tcol // c)
    cmask = jnp.logical_and(same_chunk, (tcol >= trow) if reverse else (tcol <= trow))
    beta, gcol, gpack = [], [], []
    for t_ in range(n_tiles):
        h = h0 + t_
        beta.append(jnp.sum(jnp.where(lane == d * n_heads + h, beta_all, 0.0), axis=-1, keepdims=True))
        gcol.append(jnp.sum(jnp.where(lane == (2 + d) * n_heads + h, g_all, 0.0), axis=-1, keepdims=True))
        gpack.append(jnp.where(keep, jnp.broadcast_to(gcol[t_], (tb, LANES)), 0.0))
    cum = _masked_cumsum(cmask, jnp.concatenate(gpack, axis=1))
    q_all = q_ref[0]
    k_all = k_ref[0]
    v_all = v_ref[0]
    diff, gcum, kb, vb_kbe, qg = [], [], [], [], []
    for t_ in range(n_tiles):
        ln = slice(t_ * LANES, (t_ + 1) * LANES)
        diff.append(cum[:, t_ * LANES:t_ * LANES + c])
        gcum.append(jnp.broadcast_to(cum[:, t_ * LANES + c:t_ * LANES + c + 1], (tb, LANES)))
        egc = jnp.exp(gcum[t_])
        kb.append(k_all[:, ln] * beta[t_])
        vb_kbe.append(jnp.concatenate([v_all[:, ln] * beta[t_], kb[t_] * egc], axis=1))
        qg.append(q_all[:, ln] * egc)

    chunks = list(range(n_chunks - 1, -1, -1) if reverse else range(n_chunks))
    probs = [(slice(ci * c, (ci + 1) * c), t_, slice(t_ * LANES, (t_ + 1) * LANES))
             for ci in chunks for t_ in range(n_tiles)]
    glast = [jnp.sum(jnp.broadcast_to(gcol[t_][sl], (c, LANES)), axis=0, keepdims=True) for sl, t_, _ in probs]
    kg = [(k_all[sl, ln] * jnp.exp(gl - gcum[t_][sl])).astype(BF16) for (sl, t_, ln), gl in zip(probs, glast)]
    eg = [jnp.exp(gl) for gl in glast]
    decay = [jnp.where(incl, jnp.exp(diff[t_][sl]), 0.0) for sl, t_, _ in probs]

    qkk = [_dot(jnp.concatenate([kb[t_][sl], q_all[sl, ln]], axis=0), k_all[sl, ln], NT) for sl, t_, ln in probs]
    tinv = _unit_tri_inverse([jnp.where(strict, -(x[0:c] * dc), 0.0) for x, dc in zip(qkk, decay)], eye)
    qk = [(x[c:2 * c] * dc).astype(BF16) for x, dc in zip(qkk, decay)]
    uw = [_dot(t, vb_kbe[t_][sl]) for t, (sl, t_, _) in zip(tinv, probs)]
    wq = [jnp.concatenate([x[:, LANES:2 * LANES], qg[t_][sl]], axis=0).astype(BF16)
          for x, (sl, t_, _) in zip(uw, probs)]
    dg = [_dot(kgj, x, TN) for kgj, x in zip(kg, uw)]

    s = [s_scr[t_] for t_ in range(n_tiles)]
    for j0 in range(0, len(probs), n_tiles):
        group = list(range(j0, j0 + n_tiles))
        s_in = [s[probs[j][1]] for j in group]
        gs = [_dot(dg[j][:, LANES:2 * LANES], si) for si, j in zip(s_in, group)]
        ws = [_dot(wq[j], si) for si, j in zip(s_in, group)]
        for si, x, j in zip(s_in, gs, group):
            s[probs[j][1]] = si * eg[j] - x + dg[j][:, 0:LANES]
        v_new = [uw[j][:, 0:LANES] - x[0:c] for x, j in zip(ws, group)]
        qv = [_dot(qk[j], vn) for vn, j in zip(v_new, group)]
        for x, q_, j in zip(ws, qv, group):
            sl, _, ln = probs[j]
            o_ref[0, sl, ln] = x[c:2 * c] + q_
    for t_ in range(n_tiles):
        s_scr[t_] = s[t_]

    if zero_init:
        @pl.when(i == pl.num_programs(2) - 1)
        def _():
            for t_ in range(n_tiles):
                sfin_ref[0, 0, 0, t_] = s[t_]


def _gdn_scan(qkv, proj, d, reverse, avec, dvec, s0, state_acc, layer, *, b_width, tail_off):
    b, t, _ = qkv.shape
    nh = b_width // LANES
    g = _pick(nh, (SCAN_TILES, 1))
    gw = g * LANES
    tb = _pick(t, (256, 128, 64))
    nblk = t // tb
    hb = nh // g

    def tix(i):
        return (nblk - 1 - i) if reverse else i

    def seg_spec(k):
        return pl.BlockSpec((1, tb, gw), lambda bi, h, i: (bi, tix(i), k * hb + h))

    vec_spec = pl.BlockSpec((1, LANES), lambda bi, h, i: (0, 0))
    st_spec = pl.BlockSpec((1, g, LANES, LANES), lambda bi, h, i: (bi, h, 0, 0))
    in_specs = [seg_spec(0), seg_spec(1), seg_spec(2),
                pl.BlockSpec((1, tb, LANES), lambda bi, h, i: (bi, tix(i), tail_off // LANES)),
                vec_spec, vec_spec]
    args = [qkv, qkv, qkv, proj, avec, dvec]
    o_shape = jax.ShapeDtypeStruct((b, t, b_width), F32)
    o_spec = pl.BlockSpec((1, tb, gw), lambda bi, h, i: (bi, tix(i), h))
    if s0 is not None:
        in_specs.append(st_spec)
        args.append(s0)
        out_shape, out_specs, aliases = o_shape, o_spec, {}
    else:
        in_specs.append(pl.BlockSpec(memory_space=pl.ANY))
        args.append(state_acc)
        acc_spec = pl.BlockSpec((1, 1, 1, g, LANES, LANES), lambda bi, h, i: (bi, layer, d, h, 0, 0))
        out_shape = (o_shape, jax.ShapeDtypeStruct(state_acc.shape, state_acc.dtype))
        out_specs = (o_spec, acc_spec)
        aliases = {len(args) - 1: 1}
    out = pl.pallas_call(
        functools.partial(_gdn_kernel, reverse=reverse, n_chunks=tb // CHUNK, zero_init=s0 is None, d=d, n_heads=nh,
                          n_tiles=g),
        out_shape=out_shape,
        grid=(b, hb, nblk),
        in_specs=in_specs,
        out_specs=out_specs,
        input_output_aliases=aliases,
        scratch_shapes=[pltpu.VMEM((g, LANES, LANES), F32)],
        compiler_params=_params(("parallel", "parallel", "arbitrary")),
        name="gdn_bwd" if reverse else "gdn_fwd",
    )(*args)
    return (out, None) if s0 is not None else out


def _post_kernel(yf_ref, yb_ref, za_ref, of_ref, ob_ref, zb_ref, ga_ref, gb_ref, x_ref, mod_ref,
                 gnw_ref, gnb_ref, onw_ref, gpost_ref, wpa_ref, wpb_ref, wo_ref, o_ref, ya_scr, yb_scr, *, d):
    a_width = yf_ref.shape[-1]
    b_width = of_ref.shape[-1]
    row = lax.broadcasted_iota(jnp.int32, (LANES, LANES), 0)
    col = lax.broadcasted_iota(jnp.int32, (LANES, LANES), 1)
    seg_mean = (((row // A_HEAD_DIM) == (col // A_HEAD_DIM)).astype(F32) * (1.0 / A_HEAD_DIM)).astype(BF16)
    tm = yf_ref.shape[1]
    tiles = [slice(p * LANES, (p + 1) * LANES) for p in range(a_width // LANES)]
    ys = [yf_ref[0, :, sl] + yb_ref[0, :, sl] for sl in tiles]
    y_hi = [y.astype(BF16) for y in ys]
    y_lo = [(y - h.astype(F32)).astype(BF16) for y, h in zip(ys, y_hi)]
    mean2 = [_dot(jnp.concatenate([h, l], axis=0), seg_mean) for h, l in zip(y_hi, y_lo)]
    cen = [y - (m[0:tm] + m[tm:2 * tm]) for y, m in zip(ys, mean2)]
    var = [_dot(cn * cn, seg_mean) for cn in cen]
    for sl, cn, vr in zip(tiles, cen, var):
        yn = cn * lax.rsqrt(vr + RWKV_GN_EPS) * gnw_ref[:, sl] + gnb_ref[:, sl]
        ya_scr[:, sl] = (yn * _silu(za_ref[0, :, sl])).astype(BF16)
    for hh in range(b_width // LANES):
        sl = slice(hh * LANES, (hh + 1) * LANES)
        o = of_ref[0, :, sl] + ob_ref[0, :, sl]
        on = o * lax.rsqrt(jnp.mean(o * o, axis=-1, keepdims=True) + NORM_EPS) * onw_ref[:, sl]
        yb_scr[:, sl] = (on * _silu(zb_ref[0, :, sl])).astype(BF16)
    branch_a = jnp.dot(ya_scr[...], wpa_ref[...], preferred_element_type=F32)
    branch_b = jnp.dot(yb_scr[...], wpb_ref[...], preferred_element_type=F32)
    merged = _sigmoid(ga_ref[0]) * branch_a + _sigmoid(gb_ref[0]) * branch_b
    out = jnp.dot(merged.astype(BF16), wo_ref[...], preferred_element_type=F32)
    on = out * lax.rsqrt(jnp.mean(out * out, axis=-1, keepdims=True) + NORM_EPS) * gpost_ref[...]
    o_ref[0] = x_ref[0] + mod_ref[0, :, 2 * d:3 * d] * on


def _post(yf, yb, of, ob, proj, x, mod, gnw, gnb, onw, gpost, wpa, wpb, wo, *, a_width, b_width):
    b, t, d = x.shape
    tm = _pick(t, (256, 128, 64))
    za_blk = 3
    zb_blk = (4 * a_width + 3 * b_width) // b_width
    ga_blk = (4 * a_width + 4 * b_width) // d
    assert (4 * a_width + 3 * b_width) % b_width == 0 and (4 * a_width + 4 * b_width) % d == 0

    def row_spec(width, blk=0):
        return pl.BlockSpec((1, tm, width), lambda bi, i: (bi, i, blk))

    def full(shape):
        return pl.BlockSpec(shape, lambda bi, i: (0,) * len(shape), pipeline_mode=pl.Buffered(1))

    return pl.pallas_call(
        functools.partial(_post_kernel, d=d),
        out_shape=jax.ShapeDtypeStruct((b, t, d), F32),
        grid=(b, t // tm),
        in_specs=[row_spec(a_width), row_spec(a_width), row_spec(a_width, za_blk),
                  row_spec(b_width), row_spec(b_width), row_spec(b_width, zb_blk),
                  row_spec(d, ga_blk), row_spec(d, ga_blk + 1), row_spec(d),
                  pl.BlockSpec((1, 1, 3 * d), lambda bi, i: (bi, 0, 0)),
                  full((1, a_width)), full((1, a_width)), full((1, b_width)), full((1, d)),
                  full((a_width, d)), full((b_width, d)), full((d, d))],
        out_specs=row_spec(d),
        scratch_shapes=[pltpu.VMEM((tm, a_width), BF16), pltpu.VMEM((tm, b_width), BF16)],
        compiler_params=_params(("parallel", "parallel")),
        name="post",
    )(yf, yb, proj, of, ob, proj, proj, proj, x, mod, gnw, gnb, onw, gpost, wpa, wpb, wo)


def _grid_transpose(x, rows, cols):
    b, t, d = x.shape
    return x.reshape(b, rows, cols, d).swapaxes(1, 2).reshape(b, t, d)


def _pair_block_diag(s):
    b, h, n, _ = s.shape
    sp = s.reshape(b, h // 2, 2, n, n)
    eye2 = jnp.eye(2, dtype=s.dtype)
    return jnp.einsum('bpivk,ij->bpivjk', sp, eye2).reshape(b, h // 2, 2 * n, 2 * n)


def _mixer_layer(x, mod, s_rwkv0, s_delta0, lp, dims):
    a_width, b_width = dims['a_width'], dims['b_width']
    proj = _in_proj(x, mod, lp['g_pre'], lp['w_in'])
    ys, os_ = [], []
    qkv = _gdn_prep(proj, lp['conv_w'], b_width=b_width, qkv_off=4 * a_width)
    for d, rev in ((0, False), (1, True)):
        y, _ = _rwkv_scan(proj, d, rev, lp['wup_pad'], lp['aup_pad'], lp['w0'], lp['a0'], lp['k_k'], lp['k_a'],
                          lp['r_k'], s_rwkv0[:, d], None, None, a_width=a_width, lora_off=dims['lora_off'])
        o, _ = _gdn_scan(qkv, proj, d, rev, lp['avec'], lp['dvec'], s_delta0[:, d], None, None,
                         b_width=b_width, tail_off=dims['tail_off'])
        ys.append(y)
        os_.append(o)
    return _post(ys[0], ys[1], os_[0], os_[1], proj, x, mod, lp['gn_w'], lp['gn_b'], lp['onw'], lp['g_post'],
                 lp['w_pa'], lp['w_pb'], lp['w_o'], a_width=a_width, b_width=b_width)


def kernel(x_prompt, x_sample, state_rwkv, state_delta, c, c_ctx, w_mod, b_mod, g_pre, g_post, w_in, w0, w_up, a0,
           a_up, k_k, k_a, r_k, gn_w, gn_b, conv_w, a_log, dt_bias, o_norm_w, w_pa, w_pb, w_o):
    bp, tp, dm = x_prompt.shape
    bs, ts, _ = x_sample.shape
    depth = w_mod.shape[0]
    a_width = k_k.shape[-1]
    n_bh = a_log.shape[-1]
    b_width = o_norm_w.shape[-1] * n_bh
    lora = w_up.shape[2]
    n_dir = 2
    rows = ts // GRID_W
    assert o_norm_w.shape[-1] == B_HEAD_DIM and a_width % LANES == 0 and 4 * n_bh <= LANES

    sizes = (a_width,) * 4 + (lora,) * 4 + (b_width,) * 4 + (n_dir * n_bh,) * 2 + (dm,) * 2
    offs = [0]
    for sz in sizes:
        offs.append(offs[-1] + sz)
    lora_w = 4 * lora
    lora_off = 4 * a_width + 4 * b_width + 2 * dm
    tail_off = lora_off + lora_w
    tail_used = 2 * n_dir * n_bh
    assert lora_off % lora_w == 0 and tail_off % LANES == 0
    dims = dict(a_width=a_width, b_width=b_width, lora_off=lora_off, tail_off=tail_off)

    def pack_w_in(w):
        pad = jnp.zeros((w.shape[0], LANES - tail_used), w.dtype)
        return jnp.concatenate([w[:, offs[0]:offs[4]], w[:, offs[8]:offs[12]], w[:, offs[14]:offs[16]],
                                w[:, offs[4]:offs[8]], w[:, offs[12]:offs[14]], pad], axis=1).astype(BF16)

    def pad_up(w, first):
        z = jnp.zeros((n_dir, lora_w, a_width), w.dtype)
        for d in range(n_dir):
            z = z.at[d, (first + d) * lora:(first + d + 1) * lora].set(w[d])
        return z.astype(BF16)

    def tail_vec(v):
        z = jnp.zeros((1, LANES), F32)
        return z.at[0, n_dir * n_bh:2 * n_dir * n_bh].set(v.reshape(-1))

    cvec = jnp.concatenate([c_ctx[None], c, jnp.zeros((8 - 1 - bs, dm), F32)], axis=0)
    mods = _modulation(cvec, w_mod, b_mod)

    xp = x_prompt.reshape(1, bp * tp, dm)
    xs = x_sample
    new_r = jnp.zeros((bp, depth, n_dir, a_width // A_HEAD_DIM, A_HEAD_DIM, A_HEAD_DIM), F32)
    new_d = jnp.zeros((bp, depth, n_dir, n_bh, B_HEAD_DIM, B_HEAD_DIM), F32)
    for l in range(depth):
        lp = dict(
            g_pre=g_pre[l][None], g_post=g_post[l][None], w_in=pack_w_in(w_in[l]),
            wup_pad=pad_up(w_up[l], 0), aup_pad=pad_up(a_up[l], 2),
            w0=w0[l][:, None], a0=a0[l][:, None], k_k=k_k[l][None], k_a=k_a[l][None], r_k=r_k[l][None],
            gn_w=gn_w[l][None], gn_b=gn_b[l][None], conv_w=conv_w[l],
            avec=tail_vec(a_log[l]), dvec=tail_vec(dt_bias[l]),
            onw=jnp.tile(o_norm_w[l], n_bh)[None],
            w_pa=w_pa[l].astype(BF16), w_pb=w_pb[l].astype(BF16), w_o=w_o[l].astype(BF16))
        mod_ctx = mods[l, 0:1][None]
        xp3 = xp.reshape(bp, tp, dm)
        xp, new_r, new_d = _mixer_layer_ctx(xp3, mod_ctx, lp, dims, new_r, new_d, l)
        mod_lat = mods[l, 1:1 + bs][:, None]
        s_r0 = jnp.stack([_pair_block_diag(state_rwkv[:, l, d]) for d in range(n_dir)], axis=1)
        s_d0 = state_delta[:, l]
        if l % 2 == 1:
            xs = _grid_transpose(xs, rows, GRID_W)
        xs = _mixer_layer(xs, mod_lat, s_r0, s_d0, lp, dims)
        if l % 2 == 1:
            xs = _grid_transpose(xs, GRID_W, rows)
    return (xp.reshape(bp, tp, dm), xs, new_r, new_d)


def _mixer_layer_ctx(x, mod, lp, dims, acc_r, acc_d, layer):
    b, t, d = x.shape
    a_width, b_width = dims['a_width'], dims['b_width']
    proj = _in_proj(x.reshape(1, b * t, d), mod, lp['g_pre'], lp['w_in']).reshape(b, t, -1)
    qkv = _gdn_prep(proj, lp['conv_w'], b_width=b_width, qkv_off=4 * a_width)
    ys, os_ = [], []
    for dd, rev in ((0, False), (1, True)):
        y, acc_r = _rwkv_scan(proj, dd, rev, lp['wup_pad'], lp['aup_pad'], lp['w0'], lp['a0'], lp['k_k'],
                              lp['k_a'], lp['r_k'], None, acc_r, layer, a_width=a_width, lora_off=dims['lora_off'])
        o, acc_d = _gdn_scan(qkv, proj, dd, rev, lp['avec'], lp['dvec'], None, acc_d, layer,
                             b_width=b_width, tail_off=dims['tail_off'])
        ys.append(y)
        os_.append(o)
    flat = lambda z: z.reshape(1, b * t, z.shape[-1])
    x_new = _post(flat(ys[0]), flat(ys[1]), flat(os_[0]), flat(os_[1]), flat(proj), flat(x), mod,
                  lp['gn_w'], lp['gn_b'], lp['onw'], lp['g_post'], lp['w_pa'], lp['w_pb'], lp['w_o'],
                  a_width=a_width, b_width=b_width)
    return x_new.reshape(b, t, d), acc_r, acc_d
```
